```python
import jax, jax.numpy as jnp
from jax import lax
import numpy as np

D_MODEL = 1024
BATCH = 8
SEQ = 4096
DEPTH = 1

CHUNK = 64
PLE_DIM = 256
D_FF = 2816
MLSTM_HEADS = 4
MLSTM_DQK = 64
MLSTM_DV = 128
CONV_WIDTH = 4
FOX_HEADS = 8
FOX_DH = 64
Q_BLOCK = 128
EPS = 1e-6

MLSTM_QK = MLSTM_HEADS * MLSTM_DQK
MLSTM_V = MLSTM_HEADS * MLSTM_DV
FOX_W = FOX_HEADS * FOX_DH
IN_SPLITS = (2 * MLSTM_QK, MLSTM_V, MLSTM_V, MLSTM_HEADS, MLSTM_HEADS,
             FOX_W, FOX_W, FOX_W, FOX_HEADS, D_MODEL, D_MODEL)
IN_WIDTH = sum(IN_SPLITS)
IN_OFFSETS = tuple(sum(IN_SPLITS[:i + 1]) for i in range(len(IN_SPLITS) - 1))

kernel_name = "hybrid_mlstm_fox_macaron_block"


def rms_norm(x, g):
    xf = x.astype(jnp.float32)
    y = xf * lax.rsqrt(jnp.mean(xf * xf, axis=-1, keepdims=True) + EPS)
    return (y * g.astype(jnp.float32)).astype(x.dtype)


def swiglu(x, w_gate, w_up, w_down):
    return (jax.nn.silu(x @ w_gate) * (x @ w_up)) @ w_down


def causal_conv(x, w, b):
    k_w = w.shape[0]
    s = x.shape[1]
    xp = jnp.pad(x, ((0, 0), (k_w - 1, 0), (0, 0)))
    return b + sum(w[j] * xp[:, j:j + s] for j in range(k_w))


def mlstm_chunkwise(q, k, v, i_pre, f_pre):
    b_, s_, n_h, d_k = q.shape
    d_v = v.shape[-1]
    nc = s_ // CHUNK

    def to_chunks(t):
        t = t.astype(jnp.float32).reshape((b_, nc, CHUNK) + t.shape[2:])
        return jnp.moveaxis(t, (1, 3), (0, 2))

    qc, kc, vc = to_chunks(q), to_chunks(k), to_chunks(v)
    ic = to_chunks(i_pre)
    fc = jax.nn.log_sigmoid(to_chunks(f_pre))
    tri = jnp.tril(jnp.ones((CHUNK, CHUNK), dtype=bool))

    def step(carry, xs):
        c_st, n_st, m_st = carry
        qq, kk, vv, ig, lf = xs
        bcum = jnp.cumsum(lf, axis=-1)
        log_d = bcum[..., :, None] - bcum[..., None, :] + ig[..., None, :]
        log_d = jnp.where(tri, log_d, -jnp.inf)
        log_inter = bcum + m_st[..., None]
        m_t = jnp.maximum(log_inter, jnp.max(log_d, axis=-1))
        dmat = jnp.exp(log_d - m_t[..., None])
        inter = jnp.exp(log_inter - m_t)
        sc = jnp.einsum('bhtd,bhsd->bhts', qq, kk) * dmat
        num = jnp.einsum('bhts,bhsv->bhtv', sc, vv) + inter[..., None] * jnp.einsum('bhtd,bhdv->bhtv', qq, c_st)
        den = jnp.sum(sc, axis=-1) + inter * jnp.einsum('bhtd,bhd->bht', qq, n_st)
        h = num / jnp.maximum(jnp.abs(den), jnp.exp(-m_t))[..., None]
        b_last = bcum[..., -1]
        log_w = b_last[..., None] - bcum + ig
        m_new = jnp.maximum(b_last + m_st, jnp.max(log_w, axis=-1))
        w = jnp.exp(log_w - m_new[..., None])
        decay = jnp.exp(b_last + m_st - m_new)
        c_new = decay[..., None, None] * c_st + jnp.einsum('bhs,bhsd,bhsv->bhdv', w, kk, vv)
        n_new = decay[..., None] * n_st + jnp.einsum('bhs,bhsd->bhd', w, kk)
        return (c_new, n_new, m_new), h

    init = (jnp.zeros((b_, n_h, d_k, d_v), jnp.float32),
            jnp.zeros((b_, n_h, d_k), jnp.float32),
            jnp.zeros((b_, n_h), jnp.float32))
    _, hs = lax.scan(step, init, (qc, kc, vc, ic, fc))
    return jnp.moveaxis(hs, (0, 2), (1, 3)).reshape(b_, s_, n_h, d_v)


def forgetting_attention(q, k, v, f_pre):
    b_, s_, n_h, d_h = q.shape
    nb = s_ // Q_BLOCK
    qf = q.astype(jnp.float32) * (d_h ** -0.5)
    kf = k.astype(jnp.float32)
    vf = v.astype(jnp.float32)
    c = jnp.cumsum(jax.nn.log_sigmoid(f_pre.astype(jnp.float32)), axis=1)
    c_t = jnp.transpose(c, (0, 2, 1))
    qb = jnp.moveaxis(qf.reshape(b_, nb, Q_BLOCK, n_h, d_h), 1, 0)
    cb = jnp.moveaxis(c_t.reshape(b_, n_h, nb, Q_BLOCK), 2, 0)
    qpos = jnp.arange(s_, dtype=jnp.int32).reshape(nb, Q_BLOCK)
    kpos = jnp.arange(s_, dtype=jnp.int32)

    def block(args):
        qblk, cblk, pq = args
        logits = jnp.einsum('bqhd,bkhd->bhqk', qblk, kf)
        logits = logits + cblk[..., None] - c_t[:, :, None, :]
        logits = jnp.where(pq[:, None] >= kpos[None, :], logits, -jnp.inf)
        probs = jax.nn.softmax(logits, axis=-1)
        return jnp.einsum('bhqk,bkhd->bqhd', probs, vf)

    out = lax.map(block, (qb, cb, qpos))
    return jnp.moveaxis(out, 0, 1).reshape(b_, s_, n_h * d_h)


def head_layer_norm(h, g):
    mu = jnp.mean(h, axis=-1, keepdims=True)
    var = jnp.mean(jnp.square(h - mu), axis=-1, keepdims=True)
    hn = (h - mu) * lax.rsqrt(var + EPS)
    return hn.reshape(h.shape[0], h.shape[1], -1) * g.astype(jnp.float32)


def setup_inputs(seed: int = 0) -> dict:
    key = jax.random.key(seed)
    ks = jax.random.split(key, 40)
    f32 = jnp.float32

    def dense(k, fan_in, fan_out):
        return jax.random.normal(k, (DEPTH, fan_in, fan_out), f32) * fan_in ** -0.5

    def gain(k, n):
        return 1.0 + 0.05 * jax.random.normal(k, (DEPTH, n), f32)

    def small(k, shape, s=0.02):
        return s * jax.random.normal(k, shape, f32)

    return {
        "x": jax.random.normal(ks[0], (BATCH, SEQ, D_MODEL), f32),
        "p": jax.random.normal(ks[1], (DEPTH, BATCH, SEQ, PLE_DIM), f32),
        "ffn1_pre_g": gain(ks[2], D_MODEL),
        "ffn1_w_gate": dense(ks[3], D_MODEL, D_FF),
        "ffn1_w_up": dense(ks[4], D_MODEL, D_FF),
        "ffn1_w_down": dense(ks[5], D_FF, D_MODEL),
        "ffn1_post_g": gain(ks[6], D_MODEL),
        "mix_pre_g": gain(ks[7], D_MODEL),
        "w_in": dense(ks[8], D_MODEL, IN_WIDTH),
        "conv_w": 0.5 * jax.random.normal(ks[9], (DEPTH, CONV_WIDTH, 2 * MLSTM_QK), f32),
        "conv_b": small(ks[10], (DEPTH, 2 * MLSTM_QK)),
        "mlstm_i_bias": small(ks[11], (DEPTH, MLSTM_HEADS), 0.1),
        "mlstm_f_bias": jnp.linspace(3.0, 6.0, MLSTM_HEADS, dtype=f32)[None, :] + small(ks[12], (DEPTH, MLSTM_HEADS), 0.1),
        "mlstm_norm_g": gain(ks[13], MLSTM_V),
        "fox_f_bias": jnp.linspace(1.0, 6.0, FOX_HEADS, dtype=f32)[None, :] + small(ks[14], (DEPTH, FOX_HEADS), 0.1),
        "branch_gate_bias": small(ks[15], (DEPTH, 2 * D_MODEL)),
        "w_branch_a": dense(ks[16], MLSTM_V, D_MODEL),
        "w_branch_b": dense(ks[17], FOX_W, D_MODEL),
        "w_out": dense(ks[18], D_MODEL, D_MODEL),
        "mix_post_g": gain(ks[19], D_MODEL),
        "ffn2_pre_g": gain(ks[20], D_MODEL),
        "ffn2_w_gate": dense(ks[21], D_MODEL, D_FF),
        "ffn2_w_up": dense(ks[22], D_MODEL, D_FF),
        "ffn2_w_down": dense(ks[23], D_FF, D_MODEL),
        "ffn2_post_g": gain(ks[24], D_MODEL),
        "ple_pre_g": gain(ks[25], D_MODEL),
        "ple_w_gate": dense(ks[26], D_MODEL, D_MODEL),
        "ple_b_gate": small(ks[27], (DEPTH, D_MODEL)),
        "ple_w_proj": dense(ks[28], PLE_DIM, D_MODEL),
        "ple_post_g": gain(ks[29], D_MODEL),
    }


def reference(x, p, ffn1_pre_g, ffn1_w_gate, ffn1_w_up, ffn1_w_down, ffn1_post_g,
              mix_pre_g, w_in, conv_w, conv_b, mlstm_i_bias, mlstm_f_bias, mlstm_norm_g,
              fox_f_bias, branch_gate_bias, w_branch_a, w_branch_b, w_out, mix_post_g,
              ffn2_pre_g, ffn2_w_gate, ffn2_w_up, ffn2_w_down, ffn2_post_g,
              ple_pre_g, ple_w_gate, ple_b_gate, ple_w_proj, ple_post_g):
    b_, s_, _ = x.shape
    h = x
    for layer in range(DEPTH):
        y = swiglu(rms_norm(h, ffn1_pre_g[layer]), ffn1_w_gate[layer], ffn1_w_up[layer], ffn1_w_down[layer])
        h = h + 0.5 * rms_norm(y, ffn1_post_g[layer])

        u = rms_norm(h, mix_pre_g[layer])
        z = u @ w_in[layer]
        (m_qk, m_v, m_o, m_i, m_f, f_q, f_k, f_v, f_f, g_a, g_b) = jnp.split(z, IN_OFFSETS, axis=-1)

        m_qk = jax.nn.silu(causal_conv(m_qk, conv_w[layer], conv_b[layer]))
        m_q, m_k = jnp.split(m_qk, 2, axis=-1)
        m_q = m_q.reshape(b_, s_, MLSTM_HEADS, MLSTM_DQK)
        m_k = m_k.reshape(b_, s_, MLSTM_HEADS, MLSTM_DQK) * (MLSTM_DQK ** -0.5)
        m_v = m_v.reshape(b_, s_, MLSTM_HEADS, MLSTM_DV)
        hm = mlstm_chunkwise(m_q, m_k, m_v, m_i + mlstm_i_bias[layer], m_f + mlstm_f_bias[layer])
        y_a = (jax.nn.sigmoid(m_o.astype(jnp.float32)) * head_layer_norm(hm, mlstm_norm_g[layer])).astype(x.dtype)

        y_b = forgetting_attention(f_q.reshape(b_, s_, FOX_HEADS, FOX_DH),
                                   f_k.reshape(b_, s_, FOX_HEADS, FOX_DH),
                                   f_v.reshape(b_, s_, FOX_HEADS, FOX_DH),
                                   f_f + fox_f_bias[layer]).astype(x.dtype)

        gate_a = jax.nn.sigmoid(g_a + branch_gate_bias[layer, :D_MODEL])
        gate_b = jax.nn.sigmoid(g_b + branch_gate_bias[layer, D_MODEL:])
        merged = gate_a * (y_a @ w_branch_a[layer]) + gate_b * (y_b @ w_branch_b[layer])
        h = h + rms_norm(merged @ w_out[layer], mix_post_g[layer])

        y = swiglu(rms_norm(h, ffn2_pre_g[layer]), ffn2_w_gate[layer], ffn2_w_up[layer], ffn2_w_down[layer])
        h = h + 0.5 * rms_norm(y, ffn2_post_g[layer])

        gate = jax.nn.sigmoid(rms_norm(h, ple_pre_g[layer]) @ ple_w_gate[layer] + ple_b_gate[layer])
        emb = p[layer] @ ple_w_proj[layer]
        h = h + rms_norm(gate * emb, ple_post_g[layer])
    return h
```

```python
import functools

import jax
import jax.numpy as jnp
from jax import lax
from jax.experimental import pallas as pl
from jax.experimental.pallas import tpu as pltpu

F32 = jnp.float32
BF16 = jnp.bfloat16

EPS = 1e-6
MLSTM_HEADS = 4
MLSTM_DQK = 64
MLSTM_DV = 128
MLSTM_QK = MLSTM_HEADS * MLSTM_DQK
MLSTM_V = MLSTM_HEADS * MLSTM_DV
CONV_WIDTH = 4
FOX_HEADS = 8
FOX_DH = 64
FOX_W = FOX_HEADS * FOX_DH
N_GATES = 2 * MLSTM_HEADS + FOX_HEADS

LANES = 128
SUBLANES = 8
VMEM_LIMIT_BYTES = 56 * 1024 * 1024

TOKEN_TILE = 512
FFN_CHUNK = 256
GATE_TILE = 256
MLSTM_CHUNK = 128
MLSTM_TILE = 256
FOX_TQ = 512
FOX_TK = 512


def _params(*semantics):
    return pltpu.CompilerParams(dimension_semantics=semantics, vmem_limit_bytes=VMEM_LIMIT_BYTES)


def _resident(shape):
    nd = len(shape)
    return pl.BlockSpec(shape, lambda *_: (0,) * nd, pipeline_mode=pl.Buffered(1))


def _rms(x, g):
    return x * lax.rsqrt(jnp.mean(x * x, axis=-1, keepdims=True) + EPS) * g


def _dot(a, b):
    return jnp.dot(a, b, preferred_element_type=F32)


def _dot_nt(a, b):
    return lax.dot_general(a, b, (((1,), (1,)), ((), ())), preferred_element_type=F32)


def _dot_tn(a, b):
    return lax.dot_general(a, b, (((0,), (0,)), ((), ())), preferred_element_type=F32)


def _ffn_kernel(x_ref, pre_g_ref, wg_ref, wu_ref, wd_ref, post_g_ref, o_ref):
    x = x_ref[...]
    xn = _rms(x, pre_g_ref[...]).astype(BF16)
    d_ff = wg_ref.shape[1]
    acc = jnp.zeros(x.shape, F32)
    for c in range(d_ff // FFN_CHUNK):
        sl = slice(c * FFN_CHUNK, (c + 1) * FFN_CHUNK)
        g = _dot(xn, wg_ref[:, sl])
        u = _dot(xn, wu_ref[:, sl])
        a = (g * jax.nn.sigmoid(g) * u).astype(BF16)
        acc = acc + _dot(a, wd_ref[sl, :])
    o_ref[...] = x + 0.5 * _rms(acc, post_g_ref[...])


def _ffn(h, pre_g, w_gate, w_up, w_down, post_g):
    t, d = h.shape
    d_ff = w_gate.shape[1]
    assert t % TOKEN_TILE == 0 and d_ff % FFN_CHUNK == 0
    row = pl.BlockSpec((TOKEN_TILE, d), lambda i: (i, 0))
    return pl.pallas_call(
        _ffn_kernel,
        grid=(t // TOKEN_TILE,),
        in_specs=[row, _resident((1, d)), _resident((d, d_ff)), _resident((d, d_ff)),
                  _resident((d_ff, d)), _resident((1, d))],
        out_specs=row,
        out_shape=jax.ShapeDtypeStruct((t, d), F32),
        compiler_params=_params("parallel"),
        name="ffn",
    )(h, pre_g, w_gate, w_up, w_down, post_g)


def _in_proj_kernel(h_ref, g_ref, w_ref, mqk_ref, mv_ref, fq_ref, fk_ref, fv_ref, zs_ref):
    u = _rms(h_ref[...], g_ref[...]).astype(BF16)
    z = _dot(u, w_ref[...])
    o = 2 * MLSTM_QK
    mqk_ref[...] = z[:, :o].astype(BF16)
    mv_ref[...] = z[:, o:o + MLSTM_V].astype(BF16)
    o += MLSTM_V
    fq_ref[...] = (z[:, o:o + FOX_W] * (FOX_DH ** -0.5)).astype(BF16)
    fk_ref[...] = z[:, o + FOX_W:o + 2 * FOX_W].astype(BF16)
    fv_ref[...] = z[:, o + 2 * FOX_W:o + 3 * FOX_W].astype(BF16)
    zs_ref[...] = z[:, o + 3 * FOX_W:]


def _in_proj(h, g, w):
    t, d = h.shape
    n = w.shape[1]
    row = lambda width: pl.BlockSpec((TOKEN_TILE, width), lambda i: (i, 0))
    widths = (2 * MLSTM_QK, MLSTM_V, FOX_W, FOX_W, FOX_W)
    return pl.pallas_call(
        _in_proj_kernel,
        grid=(t // TOKEN_TILE,),
        in_specs=[row(d), _resident((1, d)), _resident((d, n))],
        out_specs=[row(wd) for wd in widths] + [row(LANES)],
        out_shape=[jax.ShapeDtypeStruct((t, wd), BF16) for wd in widths]
        + [jax.ShapeDtypeStruct((t, LANES), F32)],
        compiler_params=_params("parallel"),
        name="in_proj",
    )(h, g, w)


def _split3(x):
    hi = x.astype(BF16)
    r1 = x - hi.astype(F32)
    mid = r1.astype(BF16)
    lo = (r1 - mid.astype(F32)).astype(BF16)
    return hi, mid, lo


def _gates_kernel(zs_ref, bias_ref, col_ref, row_ref, carry_ref):
    @pl.when(pl.program_id(1) == 0)
    def _():
        carry_ref[...] = jnp.zeros_like(carry_ref)

    x = zs_ref[0] + bias_ref[...]
    n = x.shape[0]
    log_f = jnp.minimum(x, 0.0) - jnp.log1p(jnp.exp(-jnp.abs(x)))
    r = lax.broadcasted_iota(jnp.int32, (n, n), 0)
    c = lax.broadcasted_iota(jnp.int32, (n, n), 1)
    tri = r >= c
    tri_all = jnp.where(tri, 1.0, 0.0).astype(BF16)
    tri_chunk = jnp.where(tri & (r // MLSTM_CHUNK == c // MLSTM_CHUNK), 1.0, 0.0).astype(BF16)
    parts = _split3(log_f)
    cum_all = sum(_dot(tri_all, p) for p in parts) + carry_ref[0:1, :]
    cum_chunk = sum(_dot(tri_chunk, p) for p in parts)
    carry_ref[0:1, :] = cum_all[n - 1:n, :]
    col = lax.broadcasted_iota(jnp.int32, x.shape, 1)
    out = jnp.where(col < MLSTM_HEADS, x, jnp.where(col < 2 * MLSTM_HEADS, cum_chunk, cum_all))
    col_ref[0] = out
    row_ref[0] = out.T[:2 * SUBLANES, :]


def _gates(zs, bias):
    b, s, _ = zs.shape
    assert s % GATE_TILE == 0 and GATE_TILE % MLSTM_CHUNK == 0
    return pl.pallas_call(
        _gates_kernel,
        grid=(b, s // GATE_TILE),
        in_specs=[pl.BlockSpec((1, GATE_TILE, LANES), lambda i, j: (i, j, 0)),
                  pl.BlockSpec((1, LANES), lambda i, j: (0, 0))],
        out_specs=[pl.BlockSpec((1, GATE_TILE, LANES), lambda i, j: (i, j, 0)),
                   pl.BlockSpec((1, 2 * SUBLANES, GATE_TILE), lambda i, j: (i, 0, j))],
        out_shape=[jax.ShapeDtypeStruct((b, s, LANES), F32),
                   jax.ShapeDtypeStruct((b, 2 * SUBLANES, s), F32)],
        scratch_shapes=[pltpu.VMEM((SUBLANES, LANES), F32)],
        compiler_params=_params("parallel", "arbitrary"),
        name="gates",
    )(zs, bias)


def _mlstm_kernel(qk_ref, v_ref, gcol_ref, grow_ref, cw_ref, cb_ref, ng_ref, o_ref,
                  tail_ref, c_ref, m_ref):
    @pl.when(pl.program_id(1) == 0)
    def _():
        tail_ref[...] = jnp.zeros_like(tail_ref)
        c_ref[...] = jnp.zeros_like(c_ref)
        m_ref[...] = jnp.zeros_like(m_ref)

    L = MLSTM_CHUNK
    rows = qk_ref.shape[1]

    cur = qk_ref[0].astype(F32)
    ext = jnp.concatenate([tail_ref[...], cur], axis=0)
    conv = cb_ref[...] + cw_ref[CONV_WIDTH - 1:CONV_WIDTH, :] * cur
    for d in range(1, CONV_WIDTH):
        shifted = pltpu.roll(ext, d, 0)[SUBLANES:, :]
        conv = conv + cw_ref[CONV_WIDTH - 1 - d:CONV_WIDTH - d, :] * shifted
    tail_ref[...] = cur[rows - SUBLANES:, :]
    qk = conv * jax.nn.sigmoid(conv)
    q_all = qk[:, :MLSTM_QK]
    k_all = qk[:, MLSTM_QK:] * (MLSTM_DQK ** -0.5)

    lane = lax.broadcasted_iota(jnp.int32, (1, LANES), 1)
    lo_half = lane < MLSTM_DQK
    srow = lax.broadcasted_iota(jnp.int32, (LANES, 1), 0) < MLSTM_DQK
    tril = (lax.broadcasted_iota(jnp.int32, (L, L), 0) >= lax.broadcasted_iota(jnp.int32, (L, L), 1))
    ones_col = jnp.where(lane == 0, 1.0, 0.0).astype(BF16)

    for ci in range(rows // L):
        rs = slice(ci * L, (ci + 1) * L)
        gcol = gcol_ref[0, rs, :]
        grow = grow_ref[0, :, rs]
        for p in range(MLSTM_HEADS // 2):
            ps = slice(p * LANES, (p + 1) * LANES)
            q_pair = q_all[rs, ps]
            k_pair = k_all[rs, ps]
            k_pair_b = k_pair.astype(BF16)
            c_pair = c_ref[p]
            c_pair_b = c_pair.astype(BF16)
            upd = []
            for hh in range(2):
                h = 2 * p + hh
                mine = lo_half if hh == 0 else jnp.logical_not(lo_half)
                icol = gcol[:, h:h + 1]
                bcol = gcol[:, MLSTM_HEADS + h:MLSTM_HEADS + h + 1]
                irow = grow[h:h + 1, :]
                brow = grow[MLSTM_HEADS + h:MLSTM_HEADS + h + 1, :]
                b_last = brow[:, L - 1:L]
                m_st = m_ref[h:h + 1, 0:1]

                log_d = jnp.where(tril, bcol - brow + irow, -jnp.inf)
                m_t = jnp.maximum(bcol + m_st, jnp.max(log_d, axis=-1, keepdims=True))
                dmat = jnp.exp(log_d - m_t)
                inter = jnp.exp(bcol + m_st - m_t)

                q_h = jnp.where(mine, q_pair, 0.0).astype(BF16)
                v_aug = jnp.concatenate(
                    [v_ref[0, rs, h * MLSTM_DV:(h + 1) * MLSTM_DV], jnp.broadcast_to(ones_col, (L, LANES))],
                    axis=1)
                sc = (_dot_nt(q_h, k_pair_b) * dmat).astype(BF16)
                nd = _dot(sc, v_aug) + inter * _dot(q_h, c_pair_b)
                den = jnp.maximum(jnp.abs(nd[:, MLSTM_DV:MLSTM_DV + 1]), jnp.exp(-m_t))
                hm = nd[:, :MLSTM_DV] / den

                mu = jnp.mean(hm, axis=-1, keepdims=True)
                cen = hm - mu
                var = jnp.mean(cen * cen, axis=-1, keepdims=True)
                gain = ng_ref[:, h * MLSTM_DV:(h + 1) * MLSTM_DV]
                o_ref[0, rs, h * MLSTM_DV:(h + 1) * MLSTM_DV] = (cen * lax.rsqrt(var + EPS) * gain).astype(o_ref.dtype)

                log_w = b_last - bcol + icol
                m_new = jnp.maximum(b_last + m_st, jnp.max(log_w, axis=0, keepdims=True))
                w = jnp.exp(log_w - m_new)
                decay = jnp.exp(b_last + m_st - m_new)
                kw = (k_pair * w).astype(BF16)
                upd.append(decay * c_pair + _dot_tn(kw, v_aug))
                m_ref[h:h + 1, :] = jnp.broadcast_to(m_new, (1, LANES))
            c_ref[p] = jnp.where(srow, upd[0], upd[1])


def _mlstm(mqk, mv, gcol, grow, conv_w, conv_b, norm_g):
    b, s, _ = mqk.shape
    assert s % MLSTM_TILE == 0 and MLSTM_TILE % MLSTM_CHUNK == 0
    slab = lambda width: pl.BlockSpec((1, MLSTM_TILE, width), lambda i, j: (i, j, 0))
    return pl.pallas_call(
        _mlstm_kernel,
        grid=(b, s // MLSTM_TILE),
        in_specs=[slab(2 * MLSTM_QK), slab(MLSTM_V), slab(LANES),
                  pl.BlockSpec((1, SUBLANES, MLSTM_TILE), lambda i, j: (i, 0, j)),
                  _resident((CONV_WIDTH, 2 * MLSTM_QK)), _resident((1, 2 * MLSTM_QK)),
                  _resident((1, MLSTM_V))],
        out_specs=slab(MLSTM_V),
        out_shape=jax.ShapeDtypeStruct((b, s, MLSTM_V), BF16),
        scratch_shapes=[pltpu.VMEM((SUBLANES, 2 * MLSTM_QK), F32),
                        pltpu.VMEM((MLSTM_HEADS // 2, LANES, 2 * LANES), F32),
                        pltpu.VMEM((SUBLANES, LANES), F32)],
        compiler_params=_params("parallel", "arbitrary"),
        name="mlstm",
    )(mqk, mv, gcol, grow, conv_w, conv_b, norm_g)


def _fox_kernel(q_ref, k_ref, v_ref, c_ref, o_ref):
    tq = q_ref.shape[1]
    i = pl.program_id(2)
    lane = lax.broadcasted_iota(jnp.int32, (1, LANES), 1)
    lo_half = lane < FOX_DH
    q = q_ref[0]
    zero = jnp.zeros_like(q)
    qs = (jnp.where(lo_half, q, zero), jnp.where(lo_half, zero, q))
    diag = pl.ds(pl.multiple_of(i * tq, tq), tq)
    c0 = [c_ref[0, hh, :, diag][:, 0:1] for hh in range(2)]

    def tile(j, carry, masked):
        ks = pl.ds(pl.multiple_of(j * FOX_TK, FOX_TK), FOX_TK)
        k = k_ref[0, ks, :]
        v = v_ref[0, ks, :]
        out = []
        for hh in range(2):
            m, l, acc = carry[hh]
            s = _dot_nt(qs[hh], k) + (c0[hh] - c_ref[0, hh, :, ks])
            if masked:
                keep = (lax.broadcasted_iota(jnp.int32, (tq, FOX_TK), 0)
                        >= lax.broadcasted_iota(jnp.int32, (tq, FOX_TK), 1))
                s = jnp.where(keep, s, -jnp.inf)
            m_new = jnp.maximum(m, jnp.max(s, axis=-1, keepdims=True))
            alpha = jnp.exp(m - m_new)
            p = jnp.exp(s - m_new)
            l = alpha * l + jnp.sum(p, axis=-1, keepdims=True)
            acc = alpha * acc + _dot(p.astype(BF16), v)
            out.append((m_new, l, acc))
        return tuple(out)

    init = tuple((jnp.full((tq, 1), -jnp.inf, F32), jnp.zeros((tq, 1), F32), jnp.zeros((tq, LANES), F32))
                 for _ in range(2))
    carry = lax.fori_loop(0, i, lambda j, c: tile(j, c, False), init)
    (_, l0, a0), (_, l1, a1) = tile(i, carry, True)
    o_ref[0] = jnp.where(lo_half, a0 / l0, a1 / l1).astype(o_ref.dtype)


def _fox(fq, fk, fv, crow):
    b, s, _ = fq.shape
    assert FOX_TQ == FOX_TK and s % FOX_TQ == 0
    pairs = FOX_HEADS // 2
    return pl.pallas_call(
        _fox_kernel,
        grid=(b, pairs, s // FOX_TQ),
        in_specs=[pl.BlockSpec((1, FOX_TQ, LANES), lambda bi, p, i: (bi, i, p)),
                  pl.BlockSpec((1, s, LANES), lambda bi, p, i: (bi, 0, p)),
                  pl.BlockSpec((1, s, LANES), lambda bi, p, i: (bi, 0, p)),
                  pl.BlockSpec((1, 2, 1, s), lambda bi, p, i: (bi, p, 0, 0))],
        out_specs=pl.BlockSpec((1, FOX_TQ, LANES), lambda bi, p, i: (bi, i, p)),
        out_shape=jax.ShapeDtypeStruct((b, s, FOX_W), BF16),
        compiler_params=_params("parallel", "parallel", "arbitrary"),
        name="fox",
    )(fq, fk, fv, crow)


def _merge_kernel(h_ref, ya_ref, yb_ref, pre_g_ref, wu_ref, bias_ref, wa_ref, wb_ref, wo_ref, post_g_ref, o_ref):
    h = h_ref[...]
    d = h.shape[1]
    u = _rms(h, pre_g_ref[...]).astype(BF16)
    zu = _dot(u, wu_ref[...])
    y_a = (jax.nn.sigmoid(zu[:, :MLSTM_V]) * ya_ref[...].astype(F32)).astype(BF16)
    gates = jax.nn.sigmoid(zu[:, MLSTM_V:] + bias_ref[...])
    merged = gates[:, :d] * _dot(y_a, wa_ref[...]) + gates[:, d:] * _dot(yb_ref[...], wb_ref[...])
    o_ref[...] = h + _rms(_dot(merged.astype(BF16), wo_ref[...]), post_g_ref[...])


def _merge(h, ya, yb, pre_g, w_u, bias, w_a, w_b, w_o, post_g):
    t, d = h.shape
    row = lambda width: pl.BlockSpec((TOKEN_TILE, width), lambda i: (i, 0))
    return pl.pallas_call(
        _merge_kernel,
        grid=(t // TOKEN_TILE,),
        in_specs=[row(d), row(MLSTM_V), row(FOX_W), _resident((1, d)), _resident(w_u.shape),
                  _resident((1, 2 * d)), _resident(w_a.shape), _resident(w_b.shape), _resident((d, d)),
                  _resident((1, d))],
        out_specs=row(d),
        out_shape=jax.ShapeDtypeStruct((t, d), F32),
        compiler_params=_params("parallel"),
        name="merge",
    )(h, ya, yb, pre_g, w_u, bias, w_a, w_b, w_o, post_g)


def _ple_kernel(h_ref, p_ref, pre_g_ref, wg_ref, bg_ref, wp_ref, post_g_ref, o_ref):
    h = h_ref[...]
    u = _rms(h, pre_g_ref[...]).astype(BF16)
    gate = jax.nn.sigmoid(_dot(u, wg_ref[...]) + bg_ref[...])
    emb = _dot(p_ref[...].astype(BF16), wp_ref[...])
    o_ref[...] = h + _rms(gate * emb, post_g_ref[...])


def _ple(h, p, pre_g, w_g, b_g, w_p, post_g):
    t, d = h.shape
    dp = p.shape[1]
    row = lambda width: pl.BlockSpec((TOKEN_TILE, width), lambda i: (i, 0))
    return pl.pallas_call(
        _ple_kernel,
        grid=(t // TOKEN_TILE,),
        in_specs=[row(d), row(dp), _resident((1, d)), _resident((d, d)), _resident((1, d)),
                  _resident((dp, d)), _resident((1, d))],
        out_specs=row(d),
        out_shape=jax.ShapeDtypeStruct((t, d), F32),
        compiler_params=_params("parallel"),
        name="ple",
    )(h, p, pre_g, w_g, b_g, w_p, post_g)


def _layer(h, p, ffn1_pre_g, ffn1_w_gate, ffn1_w_up, ffn1_w_down, ffn1_post_g,
           mix_pre_g, w_in, conv_w, conv_b, mlstm_i_bias, mlstm_f_bias, mlstm_norm_g,
           fox_f_bias, branch_gate_bias, w_branch_a, w_branch_b, w_out, mix_post_g,
           ffn2_pre_g, ffn2_w_gate, ffn2_w_up, ffn2_w_down, ffn2_post_g,
           ple_pre_g, ple_w_gate, ple_b_gate, ple_w_proj, ple_post_g, *, batch, seq):
    vec = lambda a: a.reshape(1, -1)
    bf = lambda a: a.astype(BF16)

    o_mo = 2 * MLSTM_QK + MLSTM_V
    o_mi = o_mo + MLSTM_V
    o_fq = o_mi + 2 * MLSTM_HEADS
    o_ff = o_fq + 3 * FOX_W
    o_ga = o_ff + FOX_HEADS
    w_gate_cols = jnp.concatenate([w_in[:, o_mi:o_fq], w_in[:, o_ff:o_ga]], axis=1)
    w_gate_cols = jnp.pad(w_gate_cols, ((0, 0), (0, LANES - N_GATES)))
    w_streams = bf(jnp.concatenate([w_in[:, :o_mo], w_in[:, o_fq:o_ff], w_gate_cols], axis=1))
    w_late = bf(jnp.concatenate([w_in[:, o_mo:o_mi], w_in[:, o_ga:]], axis=1))
    gate_bias = jnp.pad(jnp.concatenate([mlstm_i_bias, mlstm_f_bias, fox_f_bias]), (0, LANES - N_GATES))

    h1 = _ffn(h, vec(ffn1_pre_g), bf(ffn1_w_gate), bf(ffn1_w_up), bf(ffn1_w_down), vec(ffn1_post_g))

    mqk, mv, fq, fk, fv, zs = _in_proj(h1, vec(mix_pre_g), w_streams)
    gcol, grow = _gates(zs.reshape(batch, seq, LANES), vec(gate_bias))
    per_seq = lambda a: a.reshape(batch, seq, a.shape[-1])
    ya = _mlstm(per_seq(mqk), per_seq(mv), gcol, grow[:, :SUBLANES, :], conv_w, vec(conv_b), vec(mlstm_norm_g))
    crow = grow[:, SUBLANES:, :].reshape(batch, FOX_HEADS, 1, seq)
    yb = _fox(per_seq(fq), per_seq(fk), per_seq(fv), crow)

    h2 = _merge(h1, ya.reshape(batch * seq, MLSTM_V), yb.reshape(batch * seq, FOX_W), vec(mix_pre_g), w_late,
                vec(branch_gate_bias), bf(w_branch_a), bf(w_branch_b), bf(w_out), vec(mix_post_g))
    h3 = _ffn(h2, vec(ffn2_pre_g), bf(ffn2_w_gate), bf(ffn2_w_up), bf(ffn2_w_down), vec(ffn2_post_g))
    return _ple(h3, p, vec(ple_pre_g), bf(ple_w_gate), vec(ple_b_gate), bf(ple_w_proj), vec(ple_post_g))


def kernel(x, p, ffn1_pre_g, ffn1_w_gate, ffn1_w_up, ffn1_w_down, ffn1_post_g, mix_pre_g, w_in, conv_w, conv_b, mlstm_i_bias, mlstm_f_bias, mlstm_norm_g, fox_f_bias, branch_gate_bias, w_branch_a, w_branch_b, w_out, mix_post_g, ffn2_pre_g, ffn2_w_gate, ffn2_w_up, ffn2_w_down, ffn2_post_g, ple_pre_g, ple_w_gate, ple_b_gate, ple_w_proj, ple_post_g):
    batch, seq, d = x.shape
    weights = (ffn1_pre_g, ffn1_w_gate, ffn1_w_up, ffn1_w_down, ffn1_post_g, mix_pre_g, w_in, conv_w, conv_b,
               mlstm_i_bias, mlstm_f_bias, mlstm_norm_g, fox_f_bias, branch_gate_bias, w_branch_a, w_branch_b,
               w_out, mix_post_g, ffn2_pre_g, ffn2_w_gate, ffn2_w_up, ffn2_w_down, ffn2_post_g, ple_pre_g,
               ple_w_gate, ple_b_gate, ple_w_proj, ple_post_g)
    h = x.reshape(batch * seq, d)
    for layer in range(p.shape[0]):
        h = _layer(h, p[layer].reshape(batch * seq, -1), *(w[layer] for w in weights), batch=batch, seq=seq)
    return h.reshape(batch, seq, d)
```

```python
import math

import jax
import jax.numpy as jnp
import numpy as np
from jax import lax
from jax.experimental import pallas as pl
from jax.experimental.pallas import tpu as pltpu

F32 = jnp.float32
BF16 = jnp.bfloat16

EPS = 1e-6
MLSTM_HEADS = 4
MLSTM_DQK = 64
MLSTM_DV = 128
MLSTM_QK = MLSTM_HEADS * MLSTM_DQK
MLSTM_V = MLSTM_HEADS * MLSTM_DV
CONV_WIDTH = 4
FOX_HEADS = 8
FOX_DH = 64
FOX_W = FOX_HEADS * FOX_DH
N_GATES = 2 * MLSTM_HEADS + FOX_HEADS
LOG2E = math.log2(math.e)

LANES = 128
SUBLANES = 8
VMEM_LIMIT_BYTES = 56 * 1024 * 1024

TOKEN_TILE = 512
FFN_CHUNK = 256
GATE_TILE = 256
MLSTM_CHUNK = LANES
MLSTM_TILE = 256
MLSTM_SLOTS = 16
N_SPLIT = 3
FOX_TQ = 512
FOX_TK = 512


def _params(*semantics):
    return pltpu.CompilerParams(dimension_semantics=semantics, vmem_limit_bytes=VMEM_LIMIT_BYTES)


def _resident(shape):
    nd = len(shape)
    return pl.BlockSpec(shape, lambda *_: (0,) * nd, pipeline_mode=pl.Buffered(1))


def _rms(x, g):
    return x * lax.rsqrt(jnp.mean(x * x, axis=-1, keepdims=True) + EPS) * g


def _dot(a, b):
    return jnp.dot(a, b, preferred_element_type=F32)


def _dot_nt(a, b):
    return lax.dot_general(a, b, (((1,), (1,)), ((), ())), preferred_element_type=F32)


def _dot_tn(a, b):
    return lax.dot_general(a, b, (((0,), (0,)), ((), ())), preferred_element_type=F32)


def _ffn_kernel(x_ref, pre_g_ref, wg_ref, wu_ref, wd_ref, post_g_ref, o_ref):
    x = x_ref[...]
    xn = _rms(x, pre_g_ref[...]).astype(BF16)
    d_ff = wg_ref.shape[1]
    acc = jnp.zeros(x.shape, F32)
    for c in range(d_ff // FFN_CHUNK):
        sl = slice(c * FFN_CHUNK, (c + 1) * FFN_CHUNK)
        g = _dot(xn, wg_ref[:, sl])
        u = _dot(xn, wu_ref[:, sl])
        a = (g * jax.nn.sigmoid(g) * u).astype(BF16)
        acc = acc + _dot(a, wd_ref[sl, :])
    o_ref[...] = x + 0.5 * _rms(acc, post_g_ref[...])


def _ffn(h, pre_g, w_gate, w_up, w_down, post_g):
    t, d = h.shape
    d_ff = w_gate.shape[1]
    assert t % TOKEN_TILE == 0 and d_ff % FFN_CHUNK == 0
    row = pl.BlockSpec((TOKEN_TILE, d), lambda i: (i, 0))
    return pl.pallas_call(
        _ffn_kernel,
        grid=(t // TOKEN_TILE,),
        in_specs=[row, _resident((1, d)), _resident((d, d_ff)), _resident((d, d_ff)),
                  _resident((d_ff, d)), _resident((1, d))],
        out_specs=row,
        out_shape=jax.ShapeDtypeStruct((t, d), F32),
        compiler_params=_params("parallel"),
        name="ffn",
    )(h, pre_g, w_gate, w_up, w_down, post_g)


def _in_proj_kernel(h_ref, g_ref, w_ref, mqk_ref, mv_ref, fq_ref, fk_ref, fvt_ref, zs_ref):
    u = _rms(h_ref[...], g_ref[...]).astype(BF16)
    z = _dot(u, w_ref[...])
    o = 2 * MLSTM_QK
    mqk_ref[...] = z[:, :o].astype(BF16)
    mv_ref[...] = z[:, o:o + MLSTM_V].astype(BF16)
    o += MLSTM_V
    fq_ref[...] = (z[:, o:o + FOX_W] * (FOX_DH ** -0.5 * LOG2E)).astype(BF16)
    fk_ref[...] = z[:, o + FOX_W:o + 2 * FOX_W].astype(BF16)
    fvt_ref[0] = z[:, o + 2 * FOX_W:o + 3 * FOX_W].T.astype(BF16)
    zs_ref[...] = z[:, o + 3 * FOX_W:]


def _in_proj(h, g, w, *, batch, seq):
    t, d = h.shape
    n = w.shape[1]
    tiles = seq // TOKEN_TILE
    assert seq % TOKEN_TILE == 0
    row = lambda width: pl.BlockSpec((TOKEN_TILE, width), lambda i: (i, 0))
    widths = (2 * MLSTM_QK, MLSTM_V, FOX_W, FOX_W)
    return pl.pallas_call(
        _in_proj_kernel,
        grid=(t // TOKEN_TILE,),
        in_specs=[row(d), _resident((1, d)), _resident((d, n))],
        out_specs=[row(wd) for wd in widths]
        + [pl.BlockSpec((1, FOX_W, TOKEN_TILE), lambda i: (i // tiles, 0, i % tiles)), row(LANES)],
        out_shape=[jax.ShapeDtypeStruct((t, wd), BF16) for wd in widths]
        + [jax.ShapeDtypeStruct((batch, FOX_W, seq), BF16), jax.ShapeDtypeStruct((t, LANES), F32)],
        compiler_params=_params("parallel"),
        name="in_proj",
    )(h, g, w)


def _split3(x):
    hi = x.astype(BF16)
    r1 = x - hi.astype(F32)
    mid = r1.astype(BF16)
    lo = (r1 - mid.astype(F32)).astype(BF16)
    return hi, mid, lo


def _fox_aug_lane(head):
    return FOX_DH if head % 2 == 0 else 0


def _routing_constants():
    h_, sl = MLSTM_HEADS, MLSTM_SLOTS
    ra = np.zeros((N_SPLIT * LANES, LANES), np.float32)
    rb = np.zeros((N_SPLIT * LANES, LANES), np.float32)
    ones_a = np.zeros((1, LANES), np.float32)
    ones_b = np.zeros((1, LANES), np.float32)
    bconst = np.zeros((h_, LANES, 2 * LANES), np.float32)
    for h in range(h_):
        for j in range(N_SPLIT):
            ra[j * LANES + h_ + h, sl * h + j] = 1.0
            ra[j * LANES + h, sl * h + N_SPLIT + j] = 1.0
            rb[j * LANES + h, sl * h + 2 * N_SPLIT + j] = 1.0
            rb[j * LANES + h_ + h, sl * h + 3 * N_SPLIT + j] = -1.0
            bconst[h, sl * h + j, :LANES] = 1.0
            bconst[h, sl * h + j, LANES:] = -1.0
            bconst[h, sl * h + N_SPLIT + j, LANES:] = 1.0
        ones_a[0, sl * h + 2 * N_SPLIT:sl * h + 4 * N_SPLIT] = 1.0
        ones_b[0, sl * h:sl * h + N_SPLIT] = 1.0
    rk = np.zeros((N_SPLIT * LANES, FOX_HEADS * LANES), np.float32)
    for h in range(FOX_HEADS):
        for j in range(N_SPLIT):
            rk[j * LANES + 2 * h_ + h, LANES * h + _fox_aug_lane(h) + j] = -1.0
    as_bf = lambda a: jnp.asarray(a, BF16)
    return as_bf(ra), jnp.asarray(ones_a), as_bf(rb), jnp.asarray(ones_b), as_bf(rk), as_bf(bconst)


def _gates_kernel(zs_ref, bias_ref, fk_ref, ra_ref, ones_a_ref, rb_ref, ones_b_ref, rk_ref,
                  ga_ref, gb_ref, kaug_ref, carry_ref):
    @pl.when(pl.program_id(1) == 0)
    def _():
        carry_ref[...] = jnp.zeros_like(carry_ref)

    x = zs_ref[0] + bias_ref[...]
    n = x.shape[0]
    log_f = jnp.minimum(x, 0.0) - jnp.log1p(jnp.exp(-jnp.abs(x)))
    r = lax.broadcasted_iota(jnp.int32, (n, n), 0)
    c = lax.broadcasted_iota(jnp.int32, (n, n), 1)
    tri = r >= c
    tri_all = jnp.where(tri, 1.0, 0.0).astype(BF16)
    tri_chunk = jnp.where(tri & (r // MLSTM_CHUNK == c // MLSTM_CHUNK), 1.0, 0.0).astype(BF16)
    parts = _split3(log_f)
    cum_all = sum(_dot(tri_all, p) for p in parts) + carry_ref[0:1, :]
    cum_chunk = sum(_dot(tri_chunk, p) for p in parts)
    carry_ref[0:1, :] = cum_all[n - 1:n, :]
    col = lax.broadcasted_iota(jnp.int32, x.shape, 1)
    vals = jnp.where(col < MLSTM_HEADS, x, jnp.where(col < 2 * MLSTM_HEADS, cum_chunk, cum_all * LOG2E))

    split = jnp.concatenate(_split3(vals), axis=1)
    ga_ref[0] = (_dot(split, ra_ref[...]) + ones_a_ref[...]).astype(BF16)
    gb_ref[0] = (_dot(split, rb_ref[...]) + ones_b_ref[...]).T.astype(BF16)
    aug = _dot(split, rk_ref[...]).astype(BF16)
    lo_half = lax.broadcasted_iota(jnp.int32, (1, LANES), 1) < FOX_DH
    for h in range(FOX_HEADS):
        k_pair = fk_ref[0, :, (h // 2) * LANES:(h // 2 + 1) * LANES]
        own = lo_half if h % 2 == 0 else jnp.logical_not(lo_half)
        kaug_ref[0, :, h * LANES:(h + 1) * LANES] = jnp.where(own, k_pair, aug[:, h * LANES:(h + 1) * LANES])


def _gates(zs, bias, fk, consts):
    b, s, _ = zs.shape
    ra, ones_a, rb, ones_b, rk, _ = consts
    assert s % GATE_TILE == 0 and GATE_TILE % MLSTM_CHUNK == 0
    tile = lambda width: pl.BlockSpec((1, GATE_TILE, width), lambda i, j: (i, j, 0))
    return pl.pallas_call(
        _gates_kernel,
        grid=(b, s // GATE_TILE),
        in_specs=[tile(LANES), pl.BlockSpec((1, LANES), lambda i, j: (0, 0)), tile(FOX_W),
                  _resident(ra.shape), _resident(ones_a.shape), _resident(rb.shape), _resident(ones_b.shape),
                  _resident(rk.shape)],
        out_specs=[tile(LANES), pl.BlockSpec((1, LANES, GATE_TILE), lambda i, j: (i, 0, j)),
                   tile(FOX_HEADS * LANES)],
        out_shape=[jax.ShapeDtypeStruct((b, s, LANES), BF16),
                   jax.ShapeDtypeStruct((b, LANES, s), BF16),
                   jax.ShapeDtypeStruct((b, s, FOX_HEADS * LANES), BF16)],
        scratch_shapes=[pltpu.VMEM((SUBLANES, LANES), F32)],
        compiler_params=_params("parallel", "arbitrary"),
        name="gates",
    )(zs, bias, fk, ra, ones_a, rb, ones_b, rk)


def _mlstm_kernel(qk_ref, v_ref, ga_ref, gb_ref, bconst_ref, cw_ref, cb_ref, ng_ref, o_ref,
                  tail_ref, c_ref, m_ref):
    @pl.when(pl.program_id(1) == 0)
    def _():
        tail_ref[...] = jnp.zeros_like(tail_ref)
        c_ref[...] = jnp.zeros_like(c_ref)
        m_ref[...] = jnp.zeros_like(m_ref)

    L = MLSTM_CHUNK
    rows = qk_ref.shape[1]

    cur = qk_ref[0].astype(F32)
    ext = jnp.concatenate([tail_ref[...], cur], axis=0)
    conv = cb_ref[...] + cw_ref[CONV_WIDTH - 1:CONV_WIDTH, :] * cur
    for d in range(1, CONV_WIDTH):
        shifted = pltpu.roll(ext, d, 0)[SUBLANES:, :]
        conv = conv + cw_ref[CONV_WIDTH - 1 - d:CONV_WIDTH - d, :] * shifted
    tail_ref[...] = cur[rows - SUBLANES:, :]
    qk = conv * jax.nn.sigmoid(conv)
    q_all = qk[:, :MLSTM_QK]
    k_all = qk[:, MLSTM_QK:] * (MLSTM_DQK ** -0.5)

    lane = lax.broadcasted_iota(jnp.int32, (1, LANES), 1)
    lo_half = lane < MLSTM_DQK
    srow = lax.broadcasted_iota(jnp.int32, (LANES, 1), 0) < MLSTM_DQK
    slot_owner = lax.broadcasted_iota(jnp.int32, (LANES, 1), 0) // MLSTM_SLOTS
    tril = (lax.broadcasted_iota(jnp.int32, (L, L), 0) >= lax.broadcasted_iota(jnp.int32, (L, L), 1))
    ones_block = jnp.ones((L, LANES), BF16)
    twice = lambda a: jnp.concatenate([a, a], axis=1)

    chunks = range(rows // L)
    heads = range(MLSTM_HEADS)
    units = [(ci, h) for ci in chunks for h in heads]
    rs = {ci: slice(ci * L, (ci + 1) * L) for ci in chunks}

    b_rep, log_d, m_intra, log_w, w_max, b_last = {}, {}, {}, {}, {}, {}
    for ci, h in units:
        key_side = jnp.where(slot_owner == h, gb_ref[0, :, rs[ci]], jnp.zeros((), BF16))
        g = _dot(ga_ref[0, rs[ci], :], jnp.concatenate([key_side, bconst_ref[h]], axis=1))
        log_d[ci, h] = jnp.where(tril, g[:, :L], -jnp.inf)
        b_rep[ci, h] = g[:, L:L + LANES]
        b_last[ci, h] = b_rep[ci, h][L - 1:L, :]
        m_intra[ci, h] = jnp.max(log_d[ci, h], axis=-1, keepdims=True)
        log_w[ci, h] = b_last[ci, h] + g[:, L + LANES:]
        w_max[ci, h] = jnp.max(log_w[ci, h], axis=0, keepdims=True)

    m_st, m_next, decay = {}, {}, {}
    for h in heads:
        m = m_ref[h:h + 1, :]
        for ci in chunks:
            m_st[ci, h] = m
            m = jnp.maximum(b_last[ci, h] + m, w_max[ci, h])
            m_next[ci, h] = m
            decay[ci, h] = twice(jnp.exp(b_last[ci, h] + m_st[ci, h] - m))
        m_ref[h:h + 1, :] = m

    q_h, k_pair, k_pair_b, v_aug = {}, {}, {}, {}
    for ci in chunks:
        for p in range(MLSTM_HEADS // 2):
            ps = slice(p * LANES, (p + 1) * LANES)
            k_pair[ci, p] = k_all[rs[ci], ps]
            k_pair_b[ci, p] = k_pair[ci, p].astype(BF16)
            q_pair = q_all[rs[ci], ps]
            q_h[ci, 2 * p] = jnp.where(lo_half, q_pair, 0.0).astype(BF16)
            q_h[ci, 2 * p + 1] = jnp.where(lo_half, 0.0, q_pair).astype(BF16)
        for h in heads:
            v_aug[ci, h] = jnp.concatenate([v_ref[0, rs[ci], h * MLSTM_DV:(h + 1) * MLSTM_DV], ones_block], axis=1)

    scores = {u: _dot_nt(q_h[u], k_pair_b[u[0], u[1] // 2]) for u in units}
    upd = {}
    for ci, h in units:
        w = jnp.exp(log_w[ci, h] - m_next[ci, h])
        upd[ci, h] = _dot_tn((k_pair[ci, h // 2] * w).astype(BF16), v_aug[ci, h])

    c_b = {}
    for p in range(MLSTM_HEADS // 2):
        c_pair = c_ref[p]
        for ci in chunks:
            c_b[ci, p] = c_pair.astype(BF16)
            c_pair = jnp.where(srow, decay[ci, 2 * p] * c_pair + upd[ci, 2 * p],
                               decay[ci, 2 * p + 1] * c_pair + upd[ci, 2 * p + 1])
        c_ref[p] = c_pair

    m_t, nd = {}, {}
    for ci, h in units:
        m_t[ci, h] = jnp.maximum(b_rep[ci, h] + m_st[ci, h], m_intra[ci, h])
        sc = (scores[ci, h] * jnp.exp(log_d[ci, h] - m_t[ci, h])).astype(BF16)
        inter = jnp.exp(b_rep[ci, h] + m_st[ci, h] - m_t[ci, h])
        nd[ci, h] = _dot(sc, v_aug[ci, h]) + twice(inter) * _dot(q_h[ci, h], c_b[ci, h // 2])

    hm = {}
    for u in units:
        den = jnp.maximum(jnp.abs(nd[u][:, MLSTM_DV:]), jnp.exp(-m_t[u]))
        hm[u] = nd[u][:, :MLSTM_DV] / den
    cen = {u: hm[u] - jnp.mean(hm[u], axis=-1, keepdims=True) for u in units}
    var = {u: jnp.mean(cen[u] * cen[u], axis=-1, keepdims=True) for u in units}
    for ci, h in units:
        cols = slice(h * MLSTM_DV, (h + 1) * MLSTM_DV)
        o_ref[0, rs[ci], cols] = (cen[ci, h] * lax.rsqrt(var[ci, h] + EPS) * ng_ref[:, cols]).astype(o_ref.dtype)


def _mlstm(mqk, mv, ga, gb, bconst, conv_w, conv_b, norm_g):
    b, s, _ = mqk.shape
    assert s % MLSTM_TILE == 0 and MLSTM_TILE % MLSTM_CHUNK == 0
    assert MLSTM_DV == LANES and MLSTM_HEADS * MLSTM_SLOTS <= LANES
    slab = lambda width: pl.BlockSpec((1, MLSTM_TILE, width), lambda i, j: (i, j, 0))
    return pl.pallas_call(
        _mlstm_kernel,
        grid=(b, s // MLSTM_TILE),
        in_specs=[slab(2 * MLSTM_QK), slab(MLSTM_V), slab(LANES),
                  pl.BlockSpec((1, LANES, MLSTM_TILE), lambda i, j: (i, 0, j)),
                  _resident(bconst.shape),
                  _resident((CONV_WIDTH, 2 * MLSTM_QK)), _resident((1, 2 * MLSTM_QK)),
                  _resident((1, MLSTM_V))],
        out_specs=slab(MLSTM_V),
        out_shape=jax.ShapeDtypeStruct((b, s, MLSTM_V), BF16),
        scratch_shapes=[pltpu.VMEM((SUBLANES, 2 * MLSTM_QK), F32),
                        pltpu.VMEM((MLSTM_HEADS // 2, LANES, 2 * LANES), F32),
                        pltpu.VMEM((SUBLANES, LANES), F32)],
        compiler_params=_params("parallel", "arbitrary"),
        name="mlstm",
    )(mqk, mv, ga, gb, bconst, conv_w, conv_b, norm_g)


def _fox_kernel(q_ref, k_ref, vt_ref, o_ref):
    tq = q_ref.shape[1]
    i = pl.program_id(2)
    q = q_ref[0]
    lane = lax.broadcasted_iota(jnp.int32, (1, LANES), 1)
    lo_half = lane < FOX_DH
    q_aug = []
    for hh in range(2):
        a0 = _fox_aug_lane(hh)
        bias_lanes = jnp.where(lane >= a0, jnp.where(lane < a0 + N_SPLIT, 1.0, 0.0), 0.0).astype(BF16)
        q_aug.append(jnp.where(lo_half if hh == 0 else jnp.logical_not(lo_half), q, bias_lanes))

    def tile(j, carry, masked):
        ks = pl.ds(pl.multiple_of(j * FOX_TK, FOX_TK), FOX_TK)
        out = []
        for hh in range(2):
            m, l, acc = carry[hh]
            s = _dot_nt(k_ref[0, ks, hh * LANES:(hh + 1) * LANES], q_aug[hh])
            if masked:
                keep = (lax.broadcasted_iota(jnp.int32, (FOX_TK, tq), 0)
                        <= lax.broadcasted_iota(jnp.int32, (FOX_TK, tq), 1))
                s = jnp.where(keep, s, -jnp.inf)
            m_new = jnp.maximum(m, jnp.max(s, axis=0, keepdims=True))
            alpha = jnp.exp2(m - m_new)
            p = jnp.exp2(s - m_new)
            l = alpha * l + jnp.sum(p, axis=0, keepdims=True)
            acc = alpha * acc + _dot(vt_ref[0, hh * FOX_DH:(hh + 1) * FOX_DH, ks], p.astype(BF16))
            out.append((m_new, l, acc))
        return tuple(out)

    init = tuple((jnp.full((1, tq), -jnp.inf, F32), jnp.zeros((1, tq), F32), jnp.zeros((FOX_DH, tq), F32))
                 for _ in range(2))
    carry = lax.fori_loop(0, i, lambda j, c: tile(j, c, False), init)
    (_, l0, a0), (_, l1, a1) = tile(i, carry, True)
    o_ref[0] = jnp.concatenate([a0 / l0, a1 / l1], axis=0).T.astype(o_ref.dtype)


def _fox(fq, kaug, fvt):
    b, s, _ = fq.shape
    assert FOX_TQ == FOX_TK and s % FOX_TQ == 0 and 2 * FOX_DH == LANES
    pairs = FOX_HEADS // 2
    return pl.pallas_call(
        _fox_kernel,
        grid=(b, pairs, s // FOX_TQ),
        in_specs=[pl.BlockSpec((1, FOX_TQ, LANES), lambda bi, p, i: (bi, i, p)),
                  pl.BlockSpec((1, s, 2 * LANES), lambda bi, p, i: (bi, 0, p)),
                  pl.BlockSpec((1, LANES, s), lambda bi, p, i: (bi, p, 0))],
        out_specs=pl.BlockSpec((1, FOX_TQ, LANES), lambda bi, p, i: (bi, i, p)),
        out_shape=jax.ShapeDtypeStruct((b, s, FOX_W), BF16),
        compiler_params=_params("parallel", "parallel", "arbitrary"),
        name="fox",
    )(fq, kaug, fvt)


def _merge_kernel(h_ref, ya_ref, yb_ref, pre_g_ref, wu_ref, bias_ref, wa_ref, wb_ref, wo_ref, post_g_ref, o_ref):
    h = h_ref[...]
    d = h.shape[1]
    u = _rms(h, pre_g_ref[...]).astype(BF16)
    zu = _dot(u, wu_ref[...])
    y_a = (jax.nn.sigmoid(zu[:, :MLSTM_V]) * ya_ref[...].astype(F32)).astype(BF16)
    gates = jax.nn.sigmoid(zu[:, MLSTM_V:] + bias_ref[...])
    merged = gates[:, :d] * _dot(y_a, wa_ref[...]) + gates[:, d:] * _dot(yb_ref[...], wb_ref[...])
    o_ref[...] = h + _rms(_dot(merged.astype(BF16), wo_ref[...]), post_g_ref[...])


def _merge(h, ya, yb, pre_g, w_u, bias, w_a, w_b, w_o, post_g):
    t, d = h.shape
    row = lambda width: pl.BlockSpec((TOKEN_TILE, width), lambda i: (i, 0))
    return pl.pallas_call(
        _merge_kernel,
        grid=(t // TOKEN_TILE,),
        in_specs=[row(d), row(MLSTM_V), row(FOX_W), _resident((1, d)), _resident(w_u.shape),
                  _resident((1, 2 * d)), _resident(w_a.shape), _resident(w_b.shape), _resident((d, d)),
                  _resident((1, d))],
        out_specs=row(d),
        out_shape=jax.ShapeDtypeStruct((t, d), F32),
        compiler_params=_params("parallel"),
        name="merge",
    )(h, ya, yb, pre_g, w_u, bias, w_a, w_b, w_o, post_g)


def _ple_kernel(h_ref, p_ref, pre_g_ref, wg_ref, bg_ref, wp_ref, post_g_ref, o_ref):
    h = h_ref[...]
    u = _rms(h, pre_g_ref[...]).astype(BF16)
    gate = jax.nn.sigmoid(_dot(u, wg_ref[...]) + bg_ref[...])
    emb = _dot(p_ref[...].astype(BF16), wp_ref[...])
    o_ref[...] = h + _rms(gate * emb, post_g_ref[...])


def _ple(h, p, pre_g, w_g, b_g, w_p, post_g):
    t, d = h.shape
    dp = p.shape[1]
    row = lambda width: pl.BlockSpec((TOKEN_TILE, width), lambda i: (i, 0))
    return pl.pallas_call(
        _ple_kernel,
        grid=(t // TOKEN_TILE,),
        in_specs=[row(d), row(dp), _resident((1, d)), _resident((d, d)), _resident((1, d)),
                  _resident((dp, d)), _resident((1, d))],
        out_specs=row(d),
        out_shape=jax.ShapeDtypeStruct((t, d), F32),
        compiler_params=_params("parallel"),
        name="ple",
    )(h, p, pre_g, w_g, b_g, w_p, post_g)


def _layer(h, p, ffn1_pre_g, ffn1_w_gate, ffn1_w_up, ffn1_w_down, ffn1_post_g,
           mix_pre_g, w_in, conv_w, conv_b, mlstm_i_bias, mlstm_f_bias, mlstm_norm_g,
           fox_f_bias, branch_gate_bias, w_branch_a, w_branch_b, w_out, mix_post_g,
           ffn2_pre_g, ffn2_w_gate, ffn2_w_up, ffn2_w_down, ffn2_post_g,
           ple_pre_g, ple_w_gate, ple_b_gate, ple_w_proj, ple_post_g, *, batch, seq):
    vec = lambda a: a.reshape(1, -1)
    bf = lambda a: a.astype(BF16)

    o_mo = 2 * MLSTM_QK + MLSTM_V
    o_mi = o_mo + MLSTM_V
    o_fq = o_mi + 2 * MLSTM_HEADS
    o_ff = o_fq + 3 * FOX_W
    o_ga = o_ff + FOX_HEADS
    w_gate_cols = jnp.concatenate([w_in[:, o_mi:o_fq], w_in[:, o_ff:o_ga]], axis=1)
    w_gate_cols = jnp.pad(w_gate_cols, ((0, 0), (0, LANES - N_GATES)))
    w_streams = bf(jnp.concatenate([w_in[:, :o_mo], w_in[:, o_fq:o_ff], w_gate_cols], axis=1))
    w_late = bf(jnp.concatenate([w_in[:, o_mo:o_mi], w_in[:, o_ga:]], axis=1))
    gate_bias = jnp.pad(jnp.concatenate([mlstm_i_bias, mlstm_f_bias, fox_f_bias]), (0, LANES - N_GATES))
    consts = _routing_constants()

    h1 = _ffn(h, vec(ffn1_pre_g), bf(ffn1_w_gate), bf(ffn1_w_up), bf(ffn1_w_down), vec(ffn1_post_g))

    mqk, mv, fq, fk, fvt, zs = _in_proj(h1, vec(mix_pre_g), w_streams, batch=batch, seq=seq)
    per_seq = lambda a: a.reshape(batch, seq, a.shape[-1])
    ga, gb, kaug = _gates(per_seq(zs), vec(gate_bias), per_seq(fk), consts)
    ya = _mlstm(per_seq(mqk), per_seq(mv), ga, gb, consts[-1], conv_w, vec(conv_b), vec(mlstm_norm_g))
    yb = _fox(per_seq(fq), kaug, fvt)

    h2 = _merge(h1, ya.reshape(batch * seq, MLSTM_V), yb.reshape(batch * seq, FOX_W), vec(mix_pre_g), w_late,
                vec(branch_gate_bias), bf(w_branch_a), bf(w_branch_b), bf(w_out), vec(mix_post_g))
    h3 = _ffn(h2, vec(ffn2_pre_g), bf(ffn2_w_gate), bf(ffn2_w_up), bf(ffn2_w_down), vec(ffn2_post_g))
    return _ple(h3, p, vec(ple_pre_g), bf(ple_w_gate), vec(ple_b_gate), bf(ple_w_proj), vec(ple_post_g))


def kernel(x, p, ffn1_pre_g, ffn1_w_gate, ffn1_w_up, ffn1_w_down, ffn1_post_g, mix_pre_g, w_in, conv_w, conv_b, mlstm_i_bias, mlstm_f_bias, mlstm_norm_g, fox_f_bias, branch_gate_bias, w_branch_a, w_branch_b, w_out, mix_post_g, ffn2_pre_g, ffn2_w_gate, ffn2_w_up, ffn2_w_down, ffn2_post_g, ple_pre_g, ple_w_gate, ple_b_gate, ple_w_proj, ple_post_g):
    batch, seq, d = x.shape
    weights = (ffn1_pre_g, ffn1_w_gate, ffn1_w_up, ffn1_w_down, ffn1_post_g, mix_pre_g, w_in, conv_w, conv_b,
               mlstm_i_bias, mlstm_f_bias, mlstm_norm_g, fox_f_bias, branch_gate_bias, w_branch_a, w_branch_b,
               w_out, mix_post_g, ffn2_pre_g, ffn2_w_gate, ffn2_w_up, ffn2_w_down, ffn2_post_g, ple_pre_g,
               ple_w_gate, ple_b_gate, ple_w_proj, ple_post_g)
    h = x.reshape(batch * seq, d)
    for layer in range(p.shape[0]):
        h = _layer(h, p[layer].reshape(batch * seq, -1), *(w[layer] for w in weights), batch=batch, seq=seq)
    return h.reshape(batch, seq, d)
```

```python
import math

import jax
import jax.numpy as jnp
import numpy as np
from jax import lax
from jax.experimental import pallas as pl
from jax.experimental.pallas import tpu as pltpu

F32 = jnp.float32
BF16 = jnp.bfloat16

EPS = 1e-6
MLSTM_HEADS = 4
MLSTM_DQK = 64
MLSTM_DV = 128
MLSTM_QK = MLSTM_HEADS * MLSTM_DQK
MLSTM_V = MLSTM_HEADS * MLSTM_DV
CONV_WIDTH = 4
FOX_HEADS = 8
FOX_DH = 64
FOX_W = FOX_HEADS * FOX_DH
N_GATES = 2 * MLSTM_HEADS + FOX_HEADS
LOG2E = math.log2(math.e)

LANES = 128
SUBLANES = 8
VMEM_LIMIT_BYTES = 56 * 1024 * 1024

TOKEN_TILE = 512
FFN_CHUNK = 256
GATE_TILE = 1024
MLSTM_CHUNK = LANES
MLSTM_TILE = 256
MLSTM_SLOTS = 16
N_SPLIT = 3
FOX_TQ = 512
FOX_TK = 2 * FOX_TQ
FOX_AUG_STRIDE = 8


def _params(*semantics, flags=None):
    return pltpu.CompilerParams(dimension_semantics=semantics, vmem_limit_bytes=VMEM_LIMIT_BYTES, flags=flags)


def _resident(shape):
    nd = len(shape)
    return pl.BlockSpec(shape, lambda *_: (0,) * nd, pipeline_mode=pl.Buffered(1))


def _rms(x, g):
    return x * lax.rsqrt(jnp.mean(x * x, axis=-1, keepdims=True) + EPS) * g


def _dot(a, b):
    return jnp.dot(a, b, preferred_element_type=F32)


def _dot_nt(a, b):
    return lax.dot_general(a, b, (((1,), (1,)), ((), ())), preferred_element_type=F32)


def _dot_tn(a, b):
    return lax.dot_general(a, b, (((0,), (0,)), ((), ())), preferred_element_type=F32)


def _ffn_half_step(x, pre_g_ref, wg_ref, wu_ref, wd_ref, post_g_ref):
    xn = _rms(x, pre_g_ref[...]).astype(BF16)
    d_ff = wg_ref.shape[1]
    acc = jnp.zeros(x.shape, F32)
    for c in range(d_ff // FFN_CHUNK):
        sl = slice(c * FFN_CHUNK, (c + 1) * FFN_CHUNK)
        g = _dot(xn, wg_ref[:, sl])
        u = _dot(xn, wu_ref[:, sl])
        a = (g * jax.nn.sigmoid(g) * u).astype(BF16)
        acc = acc + _dot(a, wd_ref[sl, :])
    return x + 0.5 * _rms(acc, post_g_ref[...])


def _ffn_in_proj_kernel(x_ref, pre_g_ref, wg_ref, wu_ref, wd_ref, post_g_ref, mix_g_ref, w_ref,
                        h_ref, mqk_ref, mv_ref, fq_ref, fk_ref, fvt_ref, zs_ref):
    h = _ffn_half_step(x_ref[...], pre_g_ref, wg_ref, wu_ref, wd_ref, post_g_ref)
    h_ref[...] = h
    u = _rms(h, mix_g_ref[...]).astype(BF16)
    z = _dot(u, w_ref[...])
    o = 2 * MLSTM_QK
    mqk_ref[...] = z[:, :o].astype(BF16)
    mv_ref[...] = z[:, o:o + MLSTM_V].astype(BF16)
    o += MLSTM_V
    fq_ref[...] = (z[:, o:o + FOX_W] * (FOX_DH ** -0.5 * LOG2E)).astype(BF16)
    fk_ref[...] = z[:, o + FOX_W:o + 2 * FOX_W].astype(BF16)
    fvt_ref[0] = z[:, o + 2 * FOX_W:o + 3 * FOX_W].T.astype(BF16)
    zs_ref[...] = z[:, o + 3 * FOX_W:]


def _ffn_in_proj(x, ffn, mix_g, w, *, batch, seq):
    t, d = x.shape
    pre_g, w_gate, w_up, w_down, post_g = ffn
    d_ff = w_gate.shape[1]
    n = w.shape[1]
    tiles = seq // TOKEN_TILE
    assert seq % TOKEN_TILE == 0 and d_ff % FFN_CHUNK == 0
    row = lambda width: pl.BlockSpec((TOKEN_TILE, width), lambda i: (i, 0))
    widths = (2 * MLSTM_QK, MLSTM_V, FOX_W, FOX_W)
    return pl.pallas_call(
        _ffn_in_proj_kernel,
        grid=(t // TOKEN_TILE,),
        in_specs=[row(d), _resident((1, d)), _resident((d, d_ff)), _resident((d, d_ff)),
                  _resident((d_ff, d)), _resident((1, d)), _resident((1, d)), _resident((d, n))],
        out_specs=[row(d)] + [row(wd) for wd in widths]
        + [pl.BlockSpec((1, FOX_W, TOKEN_TILE), lambda i: (i // tiles, 0, i % tiles)), row(LANES)],
        out_shape=[jax.ShapeDtypeStruct((t, d), F32)] + [jax.ShapeDtypeStruct((t, wd), BF16) for wd in widths]
        + [jax.ShapeDtypeStruct((batch, FOX_W, seq), BF16), jax.ShapeDtypeStruct((t, LANES), F32)],
        compiler_params=_params("parallel"),
        name="ffn_in_proj",
    )(x, pre_g, w_gate, w_up, w_down, post_g, mix_g, w)


def _ffn_ple_kernel(x_ref, pre_g_ref, wg_ref, wu_ref, wd_ref, post_g_ref,
                    p_ref, ple_pre_g_ref, ple_wg_ref, ple_bg_ref, ple_wp_ref, ple_post_g_ref, o_ref):
    h = _ffn_half_step(x_ref[...], pre_g_ref, wg_ref, wu_ref, wd_ref, post_g_ref)
    u = _rms(h, ple_pre_g_ref[...]).astype(BF16)
    gate = jax.nn.sigmoid(_dot(u, ple_wg_ref[...]) + ple_bg_ref[...])
    emb = _dot(p_ref[...].astype(BF16), ple_wp_ref[...])
    o_ref[...] = h + _rms(gate * emb, ple_post_g_ref[...])


def _ffn_ple(x, ffn, p, ple):
    t, d = x.shape
    pre_g, w_gate, w_up, w_down, post_g = ffn
    ple_pre_g, ple_w_g, ple_b_g, ple_w_p, ple_post_g = ple
    d_ff = w_gate.shape[1]
    dp = p.shape[1]
    assert t % TOKEN_TILE == 0 and d_ff % FFN_CHUNK == 0
    row = lambda width: pl.BlockSpec((TOKEN_TILE, width), lambda i: (i, 0))
    return pl.pallas_call(
        _ffn_ple_kernel,
        grid=(t // TOKEN_TILE,),
        in_specs=[row(d), _resident((1, d)), _resident((d, d_ff)), _resident((d, d_ff)),
                  _resident((d_ff, d)), _resident((1, d)),
                  row(dp), _resident((1, d)), _resident((d, d)), _resident((1, d)), _resident((dp, d)),
                  _resident((1, d))],
        out_specs=row(d),
        out_shape=jax.ShapeDtypeStruct((t, d), F32),
        compiler_params=_params("parallel"),
        name="ffn_ple",
    )(x, pre_g, w_gate, w_up, w_down, post_g, p, ple_pre_g, ple_w_g, ple_b_g, ple_w_p, ple_post_g)


def _split3(x):
    hi = x.astype(BF16)
    r1 = x - hi.astype(F32)
    mid = r1.astype(BF16)
    lo = (r1 - mid.astype(F32)).astype(BF16)
    return hi, mid, lo


def _fox_aug_lane(hh, pair):
    return (FOX_DH if hh == 0 else 0) + FOX_AUG_STRIDE * pair


def _routing_constants():
    h_, sl = MLSTM_HEADS, MLSTM_SLOTS
    ra = np.zeros((N_SPLIT * LANES, LANES), np.float32)
    rb = np.zeros((N_SPLIT * LANES, LANES), np.float32)
    ones_a = np.zeros((1, LANES), np.float32)
    ones_b = np.zeros((1, LANES), np.float32)
    bconst = np.zeros((h_, LANES, 2 * LANES), np.float32)
    for h in range(h_):
        for j in range(N_SPLIT):
            ra[j * LANES + h_ + h, sl * h + j] = 1.0
            ra[j * LANES + h, sl * h + N_SPLIT + j] = 1.0
            rb[j * LANES + h, sl * h + 2 * N_SPLIT + j] = 1.0
            rb[j * LANES + h_ + h, sl * h + 3 * N_SPLIT + j] = -1.0
            bconst[h, sl * h + j, :LANES] = 1.0
            bconst[h, sl * h + j, LANES:] = -1.0
            bconst[h, sl * h + N_SPLIT + j, LANES:] = 1.0
        ones_a[0, sl * h + 2 * N_SPLIT:sl * h + 4 * N_SPLIT] = 1.0
        ones_b[0, sl * h:sl * h + N_SPLIT] = 1.0
    rk = np.zeros((N_SPLIT * LANES, LANES), np.float32)
    for h in range(FOX_HEADS):
        for j in range(N_SPLIT):
            rk[j * LANES + 2 * h_ + h, _fox_aug_lane(h % 2, h // 2) + j] = -1.0
    as_bf = lambda a: jnp.asarray(a, BF16)
    return as_bf(ra), jnp.asarray(ones_a), as_bf(rb), jnp.asarray(ones_b), as_bf(rk), as_bf(bconst)


def _gates_kernel(zs_ref, bias_ref, fk_ref, ra_ref, ones_a_ref, rb_ref, ones_b_ref, rk_ref,
                  ga_ref, gb_ref, kaug_ref, carry_ref):
    @pl.when(pl.program_id(1) == 0)
    def _():
        carry_ref[...] = jnp.zeros_like(carry_ref)

    x = zs_ref[0] + bias_ref[...]
    n = x.shape[0]
    log_f = jnp.minimum(x, 0.0) - jnp.log1p(jnp.exp(-jnp.abs(x)))
    L = MLSTM_CHUNK
    tri = jnp.where(lax.broadcasted_iota(jnp.int32, (L, L), 0) >= lax.broadcasted_iota(jnp.int32, (L, L), 1),
                    1.0, 0.0).astype(BF16)
    parts = _split3(log_f)
    cum_chunk = jnp.concatenate(
        [sum(_dot(tri, part[ci * L:(ci + 1) * L, :]) for part in parts) for ci in range(n // L)], axis=0)
    row = lax.broadcasted_iota(jnp.int32, x.shape, 0)
    before = carry_ref[0:1, :]
    offset = jnp.broadcast_to(before, x.shape)
    for ci in range(1, n // MLSTM_CHUNK):
        before = before + cum_chunk[ci * MLSTM_CHUNK - 1:ci * MLSTM_CHUNK, :]
        offset = jnp.where(row >= ci * MLSTM_CHUNK, before, offset)
    cum_all = cum_chunk + offset
    carry_ref[0:1, :] = cum_all[n - 1:n, :]
    col = lax.broadcasted_iota(jnp.int32, x.shape, 1)
    vals = jnp.where(col < MLSTM_HEADS, x, jnp.where(col < 2 * MLSTM_HEADS, cum_chunk, cum_all * LOG2E))

    split = jnp.concatenate(_split3(vals), axis=1)
    ga_ref[0] = (_dot(split, ra_ref[...]) + ones_a_ref[...]).astype(BF16)
    gb_ref[0] = (_dot(split, rb_ref[...]) + ones_b_ref[...]).T.astype(BF16)
    aug = _dot(split, rk_ref[...])
    lane = lax.broadcasted_iota(jnp.int32, (1, LANES), 1)
    lo_half = lane < FOX_DH
    for h in range(FOX_HEADS):
        a0 = _fox_aug_lane(h % 2, h // 2)
        mine = jnp.where(lane >= a0, jnp.where(lane < a0 + N_SPLIT, aug, 0.0), 0.0).astype(BF16)
        k_pair = fk_ref[0, :, (h // 2) * LANES:(h // 2 + 1) * LANES]
        own = lo_half if h % 2 == 0 else jnp.logical_not(lo_half)
        kaug_ref[0, :, h * LANES:(h + 1) * LANES] = jnp.where(own, k_pair, mine)


def _gates(zs, bias, fk, consts):
    b, s, _ = zs.shape
    ra, ones_a, rb, ones_b, rk, _ = consts
    assert s % GATE_TILE == 0 and GATE_TILE % MLSTM_CHUNK == 0
    tile = lambda width: pl.BlockSpec((1, GATE_TILE, width), lambda i, j: (i, j, 0))
    return pl.pallas_call(
        _gates_kernel,
        grid=(b, s // GATE_TILE),
        in_specs=[tile(LANES), pl.BlockSpec((1, LANES), lambda i, j: (0, 0)), tile(FOX_W),
                  _resident(ra.shape), _resident(ones_a.shape), _resident(rb.shape), _resident(ones_b.shape),
                  _resident(rk.shape)],
        out_specs=[tile(LANES), pl.BlockSpec((1, LANES, GATE_TILE), lambda i, j: (i, 0, j)),
                   tile(FOX_HEADS * LANES)],
        out_shape=[jax.ShapeDtypeStruct((b, s, LANES), BF16),
                   jax.ShapeDtypeStruct((b, LANES, s), BF16),
                   jax.ShapeDtypeStruct((b, s, FOX_HEADS * LANES), BF16)],
        scratch_shapes=[pltpu.VMEM((SUBLANES, LANES), F32)],
        compiler_params=_params("parallel", "arbitrary"),
        name="gates",
    )(zs, bias, fk, ra, ones_a, rb, ones_b, rk)


def _mlstm_kernel(qk_ref, v_ref, ga_ref, gb_ref, bconst_ref, cw_ref, cb_ref, ng_ref, o_ref,
                  tail_ref, c_ref, m_ref):
    @pl.when(pl.program_id(1) == 0)
    def _():
        tail_ref[...] = jnp.zeros_like(tail_ref)
        c_ref[...] = jnp.zeros_like(c_ref)
        m_ref[...] = jnp.zeros_like(m_ref)

    L = MLSTM_CHUNK
    rows = qk_ref.shape[1]

    cur = qk_ref[0].astype(F32)
    ext = jnp.concatenate([tail_ref[...], cur], axis=0)
    conv = cb_ref[...] + cw_ref[CONV_WIDTH - 1:CONV_WIDTH, :] * cur
    for d in range(1, CONV_WIDTH):
        shifted = pltpu.roll(ext, d, 0)[SUBLANES:, :]
        conv = conv + cw_ref[CONV_WIDTH - 1 - d:CONV_WIDTH - d, :] * shifted
    tail_ref[...] = cur[rows - SUBLANES:, :]
    qk = conv * jax.nn.sigmoid(conv)
    q_all = qk[:, :MLSTM_QK]
    k_all = qk[:, MLSTM_QK:] * (MLSTM_DQK ** -0.5)

    lane = lax.broadcasted_iota(jnp.int32, (1, LANES), 1)
    lo_half = lane < MLSTM_DQK
    srow = lax.broadcasted_iota(jnp.int32, (LANES, 1), 0) < MLSTM_DQK
    slot_owner = lax.broadcasted_iota(jnp.int32, (LANES, 1), 0) // MLSTM_SLOTS
    tril = (lax.broadcasted_iota(jnp.int32, (L, L), 0) >= lax.broadcasted_iota(jnp.int32, (L, L), 1))
    ones_block = jnp.ones((L, LANES), BF16)
    twice = lambda a: jnp.concatenate([a, a], axis=1)

    chunks = range(rows // L)
    heads = range(MLSTM_HEADS)
    units = [(ci, h) for ci in chunks for h in heads]
    rs = {ci: slice(ci * L, (ci + 1) * L) for ci in chunks}

    b_rep, log_d, m_intra, log_w, w_max, b_last = {}, {}, {}, {}, {}, {}
    for ci, h in units:
        key_side = jnp.where(slot_owner == h, gb_ref[0, :, rs[ci]], jnp.zeros((), BF16))
        g = _dot(ga_ref[0, rs[ci], :], jnp.concatenate([key_side, bconst_ref[h]], axis=1))
        log_d[ci, h] = jnp.where(tril, g[:, :L], -jnp.inf)
        b_rep[ci, h] = g[:, L:L + LANES]
        b_last[ci, h] = b_rep[ci, h][L - 1:L, :]
        m_intra[ci, h] = jnp.max(log_d[ci, h], axis=-1, keepdims=True)
        log_w[ci, h] = b_last[ci, h] + g[:, L + LANES:]
        w_max[ci, h] = jnp.max(log_w[ci, h], axis=0, keepdims=True)

    m_st, m_next, decay = {}, {}, {}
    for h in heads:
        m = m_ref[h:h + 1, :]
        for ci in chunks:
            m_st[ci, h] = m
            m = jnp.maximum(b_last[ci, h] + m, w_max[ci, h])
            m_next[ci, h] = m
            decay[ci, h] = twice(jnp.exp(b_last[ci, h] + m_st[ci, h] - m))
        m_ref[h:h + 1, :] = m

    q_h, k_pair, k_pair_b, v_aug = {}, {}, {}, {}
    for ci in chunks:
        for p in range(MLSTM_HEADS // 2):
            ps = slice(p * LANES, (p + 1) * LANES)
            k_pair[ci, p] = k_all[rs[ci], ps]
            k_pair_b[ci, p] = k_pair[ci, p].astype(BF16)
            q_pair = q_all[rs[ci], ps]
            q_h[ci, 2 * p] = jnp.where(lo_half, q_pair, 0.0).astype(BF16)
            q_h[ci, 2 * p + 1] = jnp.where(lo_half, 0.0, q_pair).astype(BF16)
        for h in heads:
            v_aug[ci, h] = jnp.concatenate([v_ref[0, rs[ci], h * MLSTM_DV:(h + 1) * MLSTM_DV], ones_block], axis=1)

    scores = {u: _dot_nt(q_h[u], k_pair_b[u[0], u[1] // 2]) for u in units}
    upd = {}
    for ci, h in units:
        w = jnp.exp(log_w[ci, h] - m_next[ci, h])
        upd[ci, h] = _dot_tn((k_pair[ci, h // 2] * w).astype(BF16), v_aug[ci, h])

    c_b = {}
    for p in range(MLSTM_HEADS // 2):
        c_pair = c_ref[p]
        for ci in chunks:
            c_b[ci, p] = c_pair.astype(BF16)
            c_pair = jnp.where(srow, decay[ci, 2 * p] * c_pair + upd[ci, 2 * p],
                               decay[ci, 2 * p + 1] * c_pair + upd[ci, 2 * p + 1])
        c_ref[p] = c_pair

    m_t, nd = {}, {}
    for ci, h in units:
        m_t[ci, h] = jnp.maximum(b_rep[ci, h] + m_st[ci, h], m_intra[ci, h])
        sc = (scores[ci, h] * jnp.exp(log_d[ci, h] - m_t[ci, h])).astype(BF16)
        inter = jnp.exp(b_rep[ci, h] + m_st[ci, h] - m_t[ci, h])
        nd[ci, h] = _dot(sc, v_aug[ci, h]) + twice(inter) * _dot(q_h[ci, h], c_b[ci, h // 2])

    hm = {}
    for u in units:
        den = jnp.maximum(jnp.abs(nd[u][:, MLSTM_DV:]), jnp.exp(-m_t[u]))
        hm[u] = nd[u][:, :MLSTM_DV] / den
    cen = {u: hm[u] - jnp.mean(hm[u], axis=-1, keepdims=True) for u in units}
    var = {u: jnp.mean(cen[u] * cen[u], axis=-1, keepdims=True) for u in units}
    for ci, h in units:
        cols = slice(h * MLSTM_DV, (h + 1) * MLSTM_DV)
        o_ref[0, rs[ci], cols] = (cen[ci, h] * lax.rsqrt(var[ci, h] + EPS) * ng_ref[:, cols]).astype(o_ref.dtype)


def _mlstm(mqk, mv, ga, gb, bconst, conv_w, conv_b, norm_g):
    b, s, _ = mqk.shape
    assert s % MLSTM_TILE == 0 and MLSTM_TILE % MLSTM_CHUNK == 0
    assert MLSTM_DV == LANES and MLSTM_HEADS * MLSTM_SLOTS <= LANES
    slab = lambda width: pl.BlockSpec((1, MLSTM_TILE, width), lambda i, j: (i, j, 0))
    return pl.pallas_call(
        _mlstm_kernel,
        grid=(b, s // MLSTM_TILE),
        in_specs=[slab(2 * MLSTM_QK), slab(MLSTM_V), slab(LANES),
                  pl.BlockSpec((1, LANES, MLSTM_TILE), lambda i, j: (i, 0, j)),
                  _resident(bconst.shape),
                  _resident((CONV_WIDTH, 2 * MLSTM_QK)), _resident((1, 2 * MLSTM_QK)),
                  _resident((1, MLSTM_V))],
        out_specs=slab(MLSTM_V),
        out_shape=jax.ShapeDtypeStruct((b, s, MLSTM_V), BF16),
        scratch_shapes=[pltpu.VMEM((SUBLANES, 2 * MLSTM_QK), F32),
                        pltpu.VMEM((MLSTM_HEADS // 2, LANES, 2 * LANES), F32),
                        pltpu.VMEM((SUBLANES, LANES), F32)],
        compiler_params=_params("parallel", "arbitrary"),
        name="mlstm",
    )(mqk, mv, ga, gb, bconst, conv_w, conv_b, norm_g)


def _fox_kernel(q_ref, k_ref, vt_ref, o_ref):
    tq = q_ref.shape[1]
    pair = pl.program_id(1)
    i = pl.program_id(2)
    q = q_ref[0]
    lane = lax.broadcasted_iota(jnp.int32, (1, LANES), 1)
    lo_half = lane < FOX_DH
    q_aug = []
    for hh in range(2):
        a0 = _fox_aug_lane(hh, pair)
        bias_lanes = jnp.where(lane >= a0, jnp.where(lane < a0 + N_SPLIT, 1.0, 0.0), 0.0).astype(BF16)
        q_aug.append(jnp.where(lo_half if hh == 0 else jnp.logical_not(lo_half), q, bias_lanes))

    def tile(start, width, state, masked):
        keys = pl.ds(pl.multiple_of(start, FOX_TQ), width)
        out = []
        for hh in range(2):
            m, l, acc = state[hh]
            s = _dot_nt(k_ref[0, keys, hh * LANES:(hh + 1) * LANES], q_aug[hh])
            if masked:
                key_pos = start + lax.broadcasted_iota(jnp.int32, (width, tq), 0)
                query_pos = i * tq + lax.broadcasted_iota(jnp.int32, (width, tq), 1)
                s = jnp.where(key_pos <= query_pos, s, -jnp.inf)
            m_new = jnp.maximum(m, jnp.max(s, axis=0, keepdims=True))
            alpha = jnp.exp2(m - m_new)
            p = jnp.exp2(s - m_new)
            l = alpha * l + jnp.sum(p, axis=0, keepdims=True)
            acc = alpha * acc + _dot(vt_ref[0, hh * FOX_DH:(hh + 1) * FOX_DH, keys], p.astype(BF16))
            out.append((m_new, l, acc))
        return tuple(out)

    state = tuple((jnp.full((1, tq), -jnp.inf, F32), jnp.zeros((1, tq), F32), jnp.zeros((FOX_DH, tq), F32))
                  for _ in range(2))
    n_full = (i * tq) // FOX_TK
    state = lax.fori_loop(0, n_full, lambda j, st: tile(j * FOX_TK, FOX_TK, st, False), state)
    rest = n_full * FOX_TK
    (_, l0, a0), (_, l1, a1) = lax.cond(
        rest == i * tq,
        lambda st: tile(rest, tq, st, True),
        lambda st: tile(rest, FOX_TK, st, True),
        state)
    o_ref[0] = jnp.concatenate([a0 / l0, a1 / l1], axis=0).T.astype(o_ref.dtype)


def _fox(fq, kaug, fvt):
    b, s, _ = fq.shape
    assert FOX_TK == 2 * FOX_TQ and s % FOX_TK == 0 and 2 * FOX_DH == LANES
    pairs = FOX_HEADS // 2
    return pl.pallas_call(
        _fox_kernel,
        grid=(b, pairs, s // FOX_TQ),
        in_specs=[pl.BlockSpec((1, FOX_TQ, LANES), lambda bi, p, i: (bi, i, p)),
                  pl.BlockSpec((1, s, 2 * LANES), lambda bi, p, i: (bi, 0, p)),
                  pl.BlockSpec((1, LANES, s), lambda bi, p, i: (bi, p, 0))],
        out_specs=pl.BlockSpec((1, FOX_TQ, LANES), lambda bi, p, i: (bi, i, p)),
        out_shape=jax.ShapeDtypeStruct((b, s, FOX_W), BF16),
        compiler_params=_params("parallel", "parallel", "arbitrary"),
        name="fox",
    )(fq, kaug, fvt)


def _merge_kernel(h_ref, ya_ref, yb_ref, pre_g_ref, wu_ref, bias_ref, wa_ref, wb_ref, wo_ref, post_g_ref, o_ref):
    h = h_ref[...]
    d = h.shape[1]
    u = _rms(h, pre_g_ref[...]).astype(BF16)
    zu = _dot(u, wu_ref[...])
    y_a = (jax.nn.sigmoid(zu[:, :MLSTM_V]) * ya_ref[...].astype(F32)).astype(BF16)
    gates = jax.nn.sigmoid(zu[:, MLSTM_V:] + bias_ref[...])
    merged = gates[:, :d] * _dot(y_a, wa_ref[...]) + gates[:, d:] * _dot(yb_ref[...], wb_ref[...])
    o_ref[...] = h + _rms(_dot(merged.astype(BF16), wo_ref[...]), post_g_ref[...])


def _merge(h, ya, yb, pre_g, w_u, bias, w_a, w_b, w_o, post_g):
    t, d = h.shape
    row = lambda width: pl.BlockSpec((TOKEN_TILE, width), lambda i: (i, 0))
    return pl.pallas_call(
        _merge_kernel,
        grid=(t // TOKEN_TILE,),
        in_specs=[row(d), row(MLSTM_V), row(FOX_W), _resident((1, d)), _resident(w_u.shape),
                  _resident((1, 2 * d)), _resident(w_a.shape), _resident(w_b.shape), _resident((d, d)),
                  _resident((1, d))],
        out_specs=row(d),
        out_shape=jax.ShapeDtypeStruct((t, d), F32),
        compiler_params=_params("parallel"),
        name="merge",
    )(h, ya, yb, pre_g, w_u, bias, w_a, w_b, w_o, post_g)


def _layer(h, p, ffn1_pre_g, ffn1_w_gate, ffn1_w_up, ffn1_w_down, ffn1_post_g,
           mix_pre_g, w_in, conv_w, conv_b, mlstm_i_bias, mlstm_f_bias, mlstm_norm_g,
           fox_f_bias, branch_gate_bias, w_branch_a, w_branch_b, w_out, mix_post_g,
           ffn2_pre_g, ffn2_w_gate, ffn2_w_up, ffn2_w_down, ffn2_post_g,
           ple_pre_g, ple_w_gate, ple_b_gate, ple_w_proj, ple_post_g, *, batch, seq):
    vec = lambda a: a.reshape(1, -1)
    bf = lambda a: a.astype(BF16)

    o_mo = 2 * MLSTM_QK + MLSTM_V
    o_mi = o_mo + MLSTM_V
    o_fq = o_mi + 2 * MLSTM_HEADS
    o_ff = o_fq + 3 * FOX_W
    o_ga = o_ff + FOX_HEADS
    w_gate_cols = jnp.concatenate([w_in[:, o_mi:o_fq], w_in[:, o_ff:o_ga]], axis=1)
    w_gate_cols = jnp.pad(w_gate_cols, ((0, 0), (0, LANES - N_GATES)))
    w_streams = bf(jnp.concatenate([w_in[:, :o_mo], w_in[:, o_fq:o_ff], w_gate_cols], axis=1))
    w_late = bf(jnp.concatenate([w_in[:, o_mo:o_mi], w_in[:, o_ga:]], axis=1))
    gate_bias = jnp.pad(jnp.concatenate([mlstm_i_bias, mlstm_f_bias, fox_f_bias]), (0, LANES - N_GATES))
    consts = _routing_constants()

    ffn1 = (vec(ffn1_pre_g), bf(ffn1_w_gate), bf(ffn1_w_up), bf(ffn1_w_down), vec(ffn1_post_g))
    h1, mqk, mv, fq, fk, fvt, zs = _ffn_in_proj(h, ffn1, vec(mix_pre_g), w_streams, batch=batch, seq=seq)
    per_seq = lambda a: a.reshape(batch, seq, a.shape[-1])
    ga, gb, kaug = _gates(per_seq(zs), vec(gate_bias), per_seq(fk), consts)
    ya = _mlstm(per_seq(mqk), per_seq(mv), ga, gb, consts[-1], conv_w, vec(conv_b), vec(mlstm_norm_g))
    yb = _fox(per_seq(fq), kaug, fvt)

    h2 = _merge(h1, ya.reshape(batch * seq, MLSTM_V), yb.reshape(batch * seq, FOX_W), vec(mix_pre_g), w_late,
                vec(branch_gate_bias), bf(w_branch_a), bf(w_branch_b), bf(w_out), vec(mix_post_g))
    ffn2 = (vec(ffn2_pre_g), bf(ffn2_w_gate), bf(ffn2_w_up), bf(ffn2_w_down), vec(ffn2_post_g))
    ple = (vec(ple_pre_g), bf(ple_w_gate), vec(ple_b_gate), bf(ple_w_proj), vec(ple_post_g))
    return _ffn_ple(h2, ffn2, p, ple)


def kernel(x, p, ffn1_pre_g, ffn1_w_gate, ffn1_w_up, ffn1_w_down, ffn1_post_g, mix_pre_g, w_in, conv_w, conv_b, mlstm_i_bias, mlstm_f_bias, mlstm_norm_g, fox_f_bias, branch_gate_bias, w_branch_a, w_branch_b, w_out, mix_post_g, ffn2_pre_g, ffn2_w_gate, ffn2_w_up, ffn2_w_down, ffn2_post_g, ple_pre_g, ple_w_gate, ple_b_gate, ple_w_proj, ple_post_g):
    batch, seq, d = x.shape
    weights = (ffn1_pre_g, ffn1_w_gate, ffn1_w_up, ffn1_w_down, ffn1_post_g, mix_pre_g, w_in, conv_w, conv_b,
               mlstm_i_bias, mlstm_f_bias, mlstm_norm_g, fox_f_bias, branch_gate_bias, w_branch_a, w_branch_b,
               w_out, mix_post_g, ffn2_pre_g, ffn2_w_gate, ffn2_w_up, ffn2_w_down, ffn2_post_g, ple_pre_g,
               ple_w_gate, ple_b_gate, ple_w_proj, ple_post_g)
    h = x.reshape(batch * seq, d)
    for layer in range(p.shape[0]):
        h = _layer(h, p[layer].reshape(batch * seq, -1), *(w[layer] for w in weights), batch=batch, seq=seq)
    return h.reshape(batch, seq, d)
```

```python
import math

import jax
import jax.numpy as jnp
import numpy as np
from jax import lax
from jax.experimental import pallas as pl
from jax.experimental.pallas import tpu as pltpu

F32 = jnp.float32
BF16 = jnp.bfloat16

EPS = 1e-6
MLSTM_HEADS = 4
MLSTM_DQK = 64
MLSTM_DV = 128
MLSTM_QK = MLSTM_HEADS * MLSTM_DQK
MLSTM_V = MLSTM_HEADS * MLSTM_DV
CONV_WIDTH = 4
FOX_HEADS = 8
FOX_DH = 64
FOX_W = FOX_HEADS * FOX_DH
N_GATES = 2 * MLSTM_HEADS + FOX_HEADS
LOG2E = math.log2(math.e)

LANES = 128
SUBLANES = 8
VMEM_LIMIT_BYTES = 56 * 1024 * 1024

TOKEN_TILE = 1024
FFN_CHUNK = 256
GATE_TILE = 1024
MLSTM_CHUNK = LANES
MLSTM_TILE = 512
MLSTM_SLOTS = 16
N_SPLIT = 3
FOX_TQ = 512
FOX_TK = 4 * FOX_TQ
FOX_AUG_STRIDE = 8


def _params(*semantics, flags=None):
    return pltpu.CompilerParams(dimension_semantics=semantics, vmem_limit_bytes=VMEM_LIMIT_BYTES, flags=flags)


def _resident(shape):
    nd = len(shape)
    return pl.BlockSpec(shape, lambda *_: (0,) * nd, pipeline_mode=pl.Buffered(1))


def _rms(x, g):
    return x * lax.rsqrt(jnp.mean(x * x, axis=-1, keepdims=True) + EPS) * g


def _dot(a, b):
    return jnp.dot(a, b, preferred_element_type=F32)


def _dot_nt(a, b):
    return lax.dot_general(a, b, (((1,), (1,)), ((), ())), preferred_element_type=F32)


def _dot_tn(a, b):
    return lax.dot_general(a, b, (((0,), (0,)), ((), ())), preferred_element_type=F32)


def _ffn_half_step(x, pre_g_ref, wg_ref, wu_ref, wd_ref, post_g_ref):
    xn = _rms(x, pre_g_ref[...]).astype(BF16)
    d_ff = wg_ref.shape[1]
    acc = jnp.zeros(x.shape, F32)
    for c in range(d_ff // FFN_CHUNK):
        sl = slice(c * FFN_CHUNK, (c + 1) * FFN_CHUNK)
        g = _dot(xn, wg_ref[:, sl])
        u = _dot(xn, wu_ref[:, sl])
        a = (g * jax.nn.sigmoid(g) * u).astype(BF16)
        acc = acc + _dot(a, wd_ref[sl, :])
    return x + 0.5 * _rms(acc, post_g_ref[...])


def _ffn_in_proj_kernel(x_ref, pre_g_ref, wg_ref, wu_ref, wd_ref, post_g_ref, mix_g_ref, w_ref,
                        h_ref, mqk_ref, mv_ref, fq_ref, fk_ref, fvt_ref, zs_ref):
    h = _ffn_half_step(x_ref[...], pre_g_ref, wg_ref, wu_ref, wd_ref, post_g_ref)
    h_ref[...] = h
    u = _rms(h, mix_g_ref[...]).astype(BF16)
    z = _dot(u, w_ref[...])
    o = 2 * MLSTM_QK
    mqk_ref[...] = z[:, :o].astype(BF16)
    mv_ref[...] = z[:, o:o + MLSTM_V].astype(BF16)
    o += MLSTM_V
    fq_ref[...] = (z[:, o:o + FOX_W] * (FOX_DH ** -0.5 * LOG2E)).astype(BF16)
    fk_ref[...] = z[:, o + FOX_W:o + 2 * FOX_W].astype(BF16)
    fvt_ref[0] = z[:, o + 2 * FOX_W:o + 3 * FOX_W].T.astype(BF16)
    zs_ref[...] = z[:, o + 3 * FOX_W:]


def _ffn_in_proj(x, ffn, mix_g, w, *, batch, seq):
    t, d = x.shape
    pre_g, w_gate, w_up, w_down, post_g = ffn
    d_ff = w_gate.shape[1]
    n = w.shape[1]
    tiles = seq // TOKEN_TILE
    assert seq % TOKEN_TILE == 0 and d_ff % FFN_CHUNK == 0
    row = lambda width: pl.BlockSpec((TOKEN_TILE, width), lambda i: (i, 0))
    widths = (2 * MLSTM_QK, MLSTM_V, FOX_W, FOX_W)
    return pl.pallas_call(
        _ffn_in_proj_kernel,
        grid=(t // TOKEN_TILE,),
        in_specs=[row(d), _resident((1, d)), _resident((d, d_ff)), _resident((d, d_ff)),
                  _resident((d_ff, d)), _resident((1, d)), _resident((1, d)), _resident((d, n))],
        out_specs=[row(d)] + [row(wd) for wd in widths]
        + [pl.BlockSpec((1, FOX_W, TOKEN_TILE), lambda i: (i // tiles, 0, i % tiles)), row(LANES)],
        out_shape=[jax.ShapeDtypeStruct((t, d), F32)] + [jax.ShapeDtypeStruct((t, wd), BF16) for wd in widths]
        + [jax.ShapeDtypeStruct((batch, FOX_W, seq), BF16), jax.ShapeDtypeStruct((t, LANES), F32)],
        compiler_params=_params("parallel"),
        name="ffn_in_proj",
    )(x, pre_g, w_gate, w_up, w_down, post_g, mix_g, w)


def _ffn_ple_kernel(x_ref, pre_g_ref, wg_ref, wu_ref, wd_ref, post_g_ref,
                    p_ref, ple_pre_g_ref, ple_wg_ref, ple_bg_ref, ple_wp_ref, ple_post_g_ref, o_ref):
    h = _ffn_half_step(x_ref[...], pre_g_ref, wg_ref, wu_ref, wd_ref, post_g_ref)
    u = _rms(h, ple_pre_g_ref[...]).astype(BF16)
    gate = jax.nn.sigmoid(_dot(u, ple_wg_ref[...]) + ple_bg_ref[...])
    emb = _dot(p_ref[...].astype(BF16), ple_wp_ref[...])
    o_ref[...] = h + _rms(gate * emb, ple_post_g_ref[...])


def _ffn_ple(x, ffn, p, ple):
    t, d = x.shape
    pre_g, w_gate, w_up, w_down, post_g = ffn
    ple_pre_g, ple_w_g, ple_b_g, ple_w_p, ple_post_g = ple
    d_ff = w_gate.shape[1]
    dp = p.shape[1]
    assert t % TOKEN_TILE == 0 and d_ff % FFN_CHUNK == 0
    row = lambda width: pl.BlockSpec((TOKEN_TILE, width), lambda i: (i, 0))
    return pl.pallas_call(
        _ffn_ple_kernel,
        grid=(t // TOKEN_TILE,),
        in_specs=[row(d), _resident((1, d)), _resident((d, d_ff)), _resident((d, d_ff)),
                  _resident((d_ff, d)), _resident((1, d)),
                  row(dp), _resident((1, d)), _resident((d, d)), _resident((1, d)), _resident((dp, d)),
                  _resident((1, d))],
        out_specs=row(d),
        out_shape=jax.ShapeDtypeStruct((t, d), F32),
        compiler_params=_params("parallel"),
        name="ffn_ple",
    )(x, pre_g, w_gate, w_up, w_down, post_g, p, ple_pre_g, ple_w_g, ple_b_g, ple_w_p, ple_post_g)


def _split3(x):
    hi = x.astype(BF16)
    r1 = x - hi.astype(F32)
    mid = r1.astype(BF16)
    lo = (r1 - mid.astype(F32)).astype(BF16)
    return hi, mid, lo


def _fox_aug_lane(hh, pair):
    return (FOX_DH if hh == 0 else 0) + FOX_AUG_STRIDE * pair


def _routing_constants():
    h_, sl = MLSTM_HEADS, MLSTM_SLOTS
    ra = np.zeros((N_SPLIT * LANES, LANES), np.float32)
    rb = np.zeros((N_SPLIT * LANES, LANES), np.float32)
    ones_a = np.zeros((1, LANES), np.float32)
    ones_b = np.zeros((1, LANES), np.float32)
    bconst = np.zeros((h_, LANES, 2 * LANES), np.float32)
    for h in range(h_):
        for j in range(N_SPLIT):
            ra[j * LANES + h_ + h, sl * h + j] = 1.0
            ra[j * LANES + h, sl * h + N_SPLIT + j] = 1.0
            rb[j * LANES + h, sl * h + 2 * N_SPLIT + j] = 1.0
            rb[j * LANES + h_ + h, sl * h + 3 * N_SPLIT + j] = -1.0
            bconst[h, sl * h + j, :LANES] = 1.0
            bconst[h, sl * h + j, LANES:] = -1.0
            bconst[h, sl * h + N_SPLIT + j, LANES:] = 1.0
        ones_a[0, sl * h + 2 * N_SPLIT:sl * h + 4 * N_SPLIT] = 1.0
        ones_b[0, sl * h:sl * h + N_SPLIT] = 1.0
    rk = np.zeros((N_SPLIT * LANES, LANES), np.float32)
    for h in range(FOX_HEADS):
        for j in range(N_SPLIT):
            rk[j * LANES + 2 * h_ + h, _fox_aug_lane(h % 2, h // 2) + j] = -1.0
    as_bf = lambda a: jnp.asarray(a, BF16)
    return as_bf(ra), jnp.asarray(ones_a), as_bf(rb), jnp.asarray(ones_b), as_bf(rk), as_bf(bconst)


def _gates_kernel(zs_ref, bias_ref, fk_ref, ra_ref, ones_a_ref, rb_ref, ones_b_ref, rk_ref,
                  ga_ref, gb_ref, kaug_ref, carry_ref):
    @pl.when(pl.program_id(1) == 0)
    def _():
        carry_ref[...] = jnp.zeros_like(carry_ref)

    x = zs_ref[0] + bias_ref[...]
    n = x.shape[0]
    log_f = jnp.minimum(x, 0.0) - jnp.log1p(jnp.exp(-jnp.abs(x)))
    L = MLSTM_CHUNK
    tri = jnp.where(lax.broadcasted_iota(jnp.int32, (L, L), 0) >= lax.broadcasted_iota(jnp.int32, (L, L), 1),
                    1.0, 0.0).astype(BF16)
    parts = _split3(log_f)
    cum_chunk = jnp.concatenate(
        [sum(_dot(tri, part[ci * L:(ci + 1) * L, :]) for part in parts) for ci in range(n // L)], axis=0)
    row = lax.broadcasted_iota(jnp.int32, x.shape, 0)
    before = carry_ref[0:1, :]
    offset = jnp.broadcast_to(before, x.shape)
    for ci in range(1, n // MLSTM_CHUNK):
        before = before + cum_chunk[ci * MLSTM_CHUNK - 1:ci * MLSTM_CHUNK, :]
        offset = jnp.where(row >= ci * MLSTM_CHUNK, before, offset)
    cum_all = cum_chunk + offset
    carry_ref[0:1, :] = cum_all[n - 1:n, :]
    col = lax.broadcasted_iota(jnp.int32, x.shape, 1)
    vals = jnp.where(col < MLSTM_HEADS, x, jnp.where(col < 2 * MLSTM_HEADS, cum_chunk, cum_all * LOG2E))

    split = jnp.concatenate(_split3(vals), axis=1)
    ga_ref[0] = (_dot(split, ra_ref[...]) + ones_a_ref[...]).astype(BF16)
    gb_ref[0] = (_dot(split, rb_ref[...]) + ones_b_ref[...]).T.astype(BF16)
    aug = _dot(split, rk_ref[...])
    lane = lax.broadcasted_iota(jnp.int32, (1, LANES), 1)
    lo_half = lane < FOX_DH
    for h in range(FOX_HEADS):
        a0 = _fox_aug_lane(h % 2, h // 2)
        mine = jnp.where(lane >= a0, jnp.where(lane < a0 + N_SPLIT, aug, 0.0), 0.0).astype(BF16)
        k_pair = fk_ref[0, :, (h // 2) * LANES:(h // 2 + 1) * LANES]
        own = lo_half if h % 2 == 0 else jnp.logical_not(lo_half)
        kaug_ref[0, :, h * LANES:(h + 1) * LANES] = jnp.where(own, k_pair, mine)


def _gates(zs, bias, fk, consts):
    b, s, _ = zs.shape
    ra, ones_a, rb, ones_b, rk, _ = consts
    assert s % GATE_TILE == 0 and GATE_TILE % MLSTM_CHUNK == 0
    tile = lambda width: pl.BlockSpec((1, GATE_TILE, width), lambda i, j: (i, j, 0))
    return pl.pallas_call(
        _gates_kernel,
        grid=(b, s // GATE_TILE),
        in_specs=[tile(LANES), pl.BlockSpec((1, LANES), lambda i, j: (0, 0)), tile(FOX_W),
                  _resident(ra.shape), _resident(ones_a.shape), _resident(rb.shape), _resident(ones_b.shape),
                  _resident(rk.shape)],
        out_specs=[tile(LANES), pl.BlockSpec((1, LANES, GATE_TILE), lambda i, j: (i, 0, j)),
                   tile(FOX_HEADS * LANES)],
        out_shape=[jax.ShapeDtypeStruct((b, s, LANES), BF16),
                   jax.ShapeDtypeStruct((b, LANES, s), BF16),
                   jax.ShapeDtypeStruct((b, s, FOX_HEADS * LANES), BF16)],
        scratch_shapes=[pltpu.VMEM((SUBLANES, LANES), F32)],
        compiler_params=_params("parallel", "arbitrary"),
        name="gates",
    )(zs, bias, fk, ra, ones_a, rb, ones_b, rk)


def _mlstm_kernel(qk_ref, v_ref, ga_ref, gb_ref, bconst_ref, cw_ref, cb_ref, ng_ref, o_ref,
                  tail_ref, c_ref, m_ref):
    @pl.when(pl.program_id(1) == 0)
    def _():
        tail_ref[...] = jnp.zeros_like(tail_ref)
        c_ref[...] = jnp.zeros_like(c_ref)
        m_ref[...] = jnp.zeros_like(m_ref)

    L = MLSTM_CHUNK
    rows = qk_ref.shape[1]

    cur = qk_ref[0].astype(F32)
    ext = jnp.concatenate([tail_ref[...], cur], axis=0)
    conv = cb_ref[...] + cw_ref[CONV_WIDTH - 1:CONV_WIDTH, :] * cur
    for d in range(1, CONV_WIDTH):
        shifted = pltpu.roll(ext, d, 0)[SUBLANES:, :]
        conv = conv + cw_ref[CONV_WIDTH - 1 - d:CONV_WIDTH - d, :] * shifted
    tail_ref[...] = cur[rows - SUBLANES:, :]
    qk = conv * jax.nn.sigmoid(conv)
    q_all = qk[:, :MLSTM_QK]
    k_all = qk[:, MLSTM_QK:] * (MLSTM_DQK ** -0.5)

    lane = lax.broadcasted_iota(jnp.int32, (1, LANES), 1)
    lo_half = lane < MLSTM_DQK
    srow = lax.broadcasted_iota(jnp.int32, (LANES, 1), 0) < MLSTM_DQK
    slot_owner = lax.broadcasted_iota(jnp.int32, (LANES, 1), 0) // MLSTM_SLOTS
    tril = (lax.broadcasted_iota(jnp.int32, (L, L), 0) >= lax.broadcasted_iota(jnp.int32, (L, L), 1))
    ones_block = jnp.ones((L, LANES), BF16)
    twice = lambda a: jnp.concatenate([a, a], axis=1)

    chunks = range(rows // L)
    heads = range(MLSTM_HEADS)
    units = [(ci, h) for ci in chunks for h in heads]
    rs = {ci: slice(ci * L, (ci + 1) * L) for ci in chunks}

    b_rep, log_d, m_intra, log_w, w_max, b_last = {}, {}, {}, {}, {}, {}
    for ci, h in units:
        key_side = jnp.where(slot_owner == h, gb_ref[0, :, rs[ci]], jnp.zeros((), BF16))
        g = _dot(ga_ref[0, rs[ci], :], jnp.concatenate([key_side, bconst_ref[h]], axis=1))
        log_d[ci, h] = jnp.where(tril, g[:, :L], -jnp.inf)
        b_rep[ci, h] = g[:, L:L + LANES]
        b_last[ci, h] = b_rep[ci, h][L - 1:L, :]
        m_intra[ci, h] = jnp.max(log_d[ci, h], axis=-1, keepdims=True)
        log_w[ci, h] = b_last[ci, h] + g[:, L + LANES:]
        w_max[ci, h] = jnp.max(log_w[ci, h], axis=0, keepdims=True)

    m_st, m_next, decay = {}, {}, {}
    for h in heads:
        m = m_ref[h:h + 1, :]
        for ci in chunks:
            m_st[ci, h] = m
            m = jnp.maximum(b_last[ci, h] + m, w_max[ci, h])
            m_next[ci, h] = m
            decay[ci, h] = twice(jnp.exp(b_last[ci, h] + m_st[ci, h] - m))
        m_ref[h:h + 1, :] = m

    q_h, k_pair, k_pair_b, v_aug = {}, {}, {}, {}
    for ci in chunks:
        for p in range(MLSTM_HEADS // 2):
            ps = slice(p * LANES, (p + 1) * LANES)
            k_pair[ci, p] = k_all[rs[ci], ps]
            k_pair_b[ci, p] = k_pair[ci, p].astype(BF16)
            q_pair = q_all[rs[ci], ps]
            q_h[ci, 2 * p] = jnp.where(lo_half, q_pair, 0.0).astype(BF16)
            q_h[ci, 2 * p + 1] = jnp.where(lo_half, 0.0, q_pair).astype(BF16)
        for h in heads:
            v_aug[ci, h] = jnp.concatenate([v_ref[0, rs[ci], h * MLSTM_DV:(h + 1) * MLSTM_DV], ones_block], axis=1)

    scores = {u: _dot_nt(q_h[u], k_pair_b[u[0], u[1] // 2]) for u in units}
    upd = {}
    for ci, h in units:
        w = jnp.exp(log_w[ci, h] - m_next[ci, h])
        upd[ci, h] = _dot_tn((k_pair[ci, h // 2] * w).astype(BF16), v_aug[ci, h])

    c_b = {}
    for p in range(MLSTM_HEADS // 2):
        c_pair = c_ref[p]
        for ci in chunks:
            c_b[ci, p] = c_pair.astype(BF16)
            c_pair = jnp.where(srow, decay[ci, 2 * p] * c_pair + upd[ci, 2 * p],
                               decay[ci, 2 * p + 1] * c_pair + upd[ci, 2 * p + 1])
        c_ref[p] = c_pair

    m_t, nd = {}, {}
    for ci, h in units:
        m_t[ci, h] = jnp.maximum(b_rep[ci, h] + m_st[ci, h], m_intra[ci, h])
        sc = (scores[ci, h] * jnp.exp(log_d[ci, h] - m_t[ci, h])).astype(BF16)
        inter = jnp.exp(b_rep[ci, h] + m_st[ci, h] - m_t[ci, h])
        nd[ci, h] = _dot(sc, v_aug[ci, h]) + twice(inter) * _dot(q_h[ci, h], c_b[ci, h // 2])

    hm = {}
    for u in units:
        den = jnp.maximum(jnp.abs(nd[u][:, MLSTM_DV:]), jnp.exp(-m_t[u]))
        hm[u] = nd[u][:, :MLSTM_DV] / den
    cen = {u: hm[u] - jnp.mean(hm[u], axis=-1, keepdims=True) for u in units}
    var = {u: jnp.mean(cen[u] * cen[u], axis=-1, keepdims=True) for u in units}
    for ci, h in units:
        cols = slice(h * MLSTM_DV, (h + 1) * MLSTM_DV)
        o_ref[0, rs[ci], cols] = (cen[ci, h] * lax.rsqrt(var[ci, h] + EPS) * ng_ref[:, cols]).astype(o_ref.dtype)


def _mlstm(mqk, mv, ga, gb, bconst, conv_w, conv_b, norm_g):
    b, s, _ = mqk.shape
    assert s % MLSTM_TILE == 0 and MLSTM_TILE % MLSTM_CHUNK == 0
    assert MLSTM_DV == LANES and MLSTM_HEADS * MLSTM_SLOTS <= LANES
    slab = lambda width: pl.BlockSpec((1, MLSTM_TILE, width), lambda i, j: (i, j, 0))
    return pl.pallas_call(
        _mlstm_kernel,
        grid=(b, s // MLSTM_TILE),
        in_specs=[slab(2 * MLSTM_QK), slab(MLSTM_V), slab(LANES),
                  pl.BlockSpec((1, LANES, MLSTM_TILE), lambda i, j: (i, 0, j)),
                  _resident(bconst.shape),
                  _resident((CONV_WIDTH, 2 * MLSTM_QK)), _resident((1, 2 * MLSTM_QK)),
                  _resident((1, MLSTM_V))],
        out_specs=slab(MLSTM_V),
        out_shape=jax.ShapeDtypeStruct((b, s, MLSTM_V), BF16),
        scratch_shapes=[pltpu.VMEM((SUBLANES, 2 * MLSTM_QK), F32),
                        pltpu.VMEM((MLSTM_HEADS // 2, LANES, 2 * LANES), F32),
                        pltpu.VMEM((SUBLANES, LANES), F32)],
        compiler_params=_params("parallel", "arbitrary"),
        name="mlstm",
    )(mqk, mv, ga, gb, bconst, conv_w, conv_b, norm_g)


def _fox_kernel(q_ref, k_ref, vt_ref, o_ref):
    tq = q_ref.shape[1]
    pair = pl.program_id(1)
    i = pl.program_id(2)
    q = q_ref[0]
    lane = lax.broadcasted_iota(jnp.int32, (1, LANES), 1)
    lo_half = lane < FOX_DH
    q_aug = []
    for hh in range(2):
        a0 = _fox_aug_lane(hh, pair)
        bias_lanes = jnp.where(lane >= a0, jnp.where(lane < a0 + N_SPLIT, 1.0, 0.0), 0.0).astype(BF16)
        q_aug.append(jnp.where(lo_half if hh == 0 else jnp.logical_not(lo_half), q, bias_lanes))

    def tile(start, width, state, masked):
        keys = pl.ds(pl.multiple_of(start, FOX_TQ), width)
        out = []
        for hh in range(2):
            m, l, acc = state[hh]
            s = _dot_nt(k_ref[0, keys, hh * LANES:(hh + 1) * LANES], q_aug[hh])
            if masked:
                key_pos = start + lax.broadcasted_iota(jnp.int32, (width, tq), 0)
                query_pos = i * tq + lax.broadcasted_iota(jnp.int32, (width, tq), 1)
                s = jnp.where(key_pos <= query_pos, s, -jnp.inf)
            m_new = jnp.maximum(m, jnp.max(s, axis=0, keepdims=True))
            alpha = jnp.exp2(m - m_new)
            p = jnp.exp2(s - m_new)
            l = alpha * l + jnp.sum(p, axis=0, keepdims=True)
            acc = alpha * acc + _dot(vt_ref[0, hh * FOX_DH:(hh + 1) * FOX_DH, keys], p.astype(BF16))
            out.append((m_new, l, acc))
        return tuple(out)

    state = tuple((jnp.full((1, tq), -jnp.inf, F32), jnp.zeros((1, tq), F32), jnp.zeros((FOX_DH, tq), F32))
                  for _ in range(2))
    n_full = (i * tq) // FOX_TK
    state = lax.fori_loop(0, n_full, lambda j, st: tile(j * FOX_TK, FOX_TK, st, False), state)
    rest = n_full * FOX_TK
    tails = [(lambda st, w=w: tile(rest, w, st, True)) for w in range(tq, FOX_TK + tq, tq)]
    (_, l0, a0), (_, l1, a1) = lax.switch((i * tq - rest) // tq, tails, state)
    o_ref[0] = jnp.concatenate([a0 / l0, a1 / l1], axis=0).T.astype(o_ref.dtype)


def _fox(fq, kaug, fvt):
    b, s, _ = fq.shape
    assert FOX_TK % FOX_TQ == 0 and s % FOX_TK == 0 and 2 * FOX_DH == LANES
    pairs = FOX_HEADS // 2
    return pl.pallas_call(
        _fox_kernel,
        grid=(b, pairs, s // FOX_TQ),
        in_specs=[pl.BlockSpec((1, FOX_TQ, LANES), lambda bi, p, i: (bi, i, p)),
                  pl.BlockSpec((1, s, 2 * LANES), lambda bi, p, i: (bi, 0, p)),
                  pl.BlockSpec((1, LANES, s), lambda bi, p, i: (bi, p, 0))],
        out_specs=pl.BlockSpec((1, FOX_TQ, LANES), lambda bi, p, i: (bi, i, p)),
        out_shape=jax.ShapeDtypeStruct((b, s, FOX_W), BF16),
        compiler_params=_params("parallel", "parallel", "arbitrary"),
        name="fox",
    )(fq, kaug, fvt)


def _merge_kernel(h_ref, ya_ref, yb_ref, pre_g_ref, wu_ref, bias_ref, wa_ref, wb_ref, wo_ref, post_g_ref, o_ref):
    h = h_ref[...]
    d = h.shape[1]
    u = _rms(h, pre_g_ref[...]).astype(BF16)
    zu = _dot(u, wu_ref[...])
    y_a = (jax.nn.sigmoid(zu[:, :MLSTM_V]) * ya_ref[...].astype(F32)).astype(BF16)
    gates = jax.nn.sigmoid(zu[:, MLSTM_V:] + bias_ref[...])
    merged = gates[:, :d] * _dot(y_a, wa_ref[...]) + gates[:, d:] * _dot(yb_ref[...], wb_ref[...])
    o_ref[...] = h + _rms(_dot(merged.astype(BF16), wo_ref[...]), post_g_ref[...])


def _merge(h, ya, yb, pre_g, w_u, bias, w_a, w_b, w_o, post_g):
    t, d = h.shape
    row = lambda width: pl.BlockSpec((TOKEN_TILE, width), lambda i: (i, 0))
    return pl.pallas_call(
        _merge_kernel,
        grid=(t // TOKEN_TILE,),
        in_specs=[row(d), row(MLSTM_V), row(FOX_W), _resident((1, d)), _resident(w_u.shape),
                  _resident((1, 2 * d)), _resident(w_a.shape), _resident(w_b.shape), _resident((d, d)),
                  _resident((1, d))],
        out_specs=row(d),
        out_shape=jax.ShapeDtypeStruct((t, d), F32),
        compiler_params=_params("parallel"),
        name="merge",
    )(h, ya, yb, pre_g, w_u, bias, w_a, w_b, w_o, post_g)


def _layer(h, p, ffn1_pre_g, ffn1_w_gate, ffn1_w_up, ffn1_w_down, ffn1_post_g,
           mix_pre_g, w_in, conv_w, conv_b, mlstm_i_bias, mlstm_f_bias, mlstm_norm_g,
           fox_f_bias, branch_gate_bias, w_branch_a, w_branch_b, w_out, mix_post_g,
           ffn2_pre_g, ffn2_w_gate, ffn2_w_up, ffn2_w_down, ffn2_post_g,
           ple_pre_g, ple_w_gate, ple_b_gate, ple_w_proj, ple_post_g, *, batch, seq):
    vec = lambda a: a.reshape(1, -1)
    bf = lambda a: a.astype(BF16)

    o_mo = 2 * MLSTM_QK + MLSTM_V
    o_mi = o_mo + MLSTM_V
    o_fq = o_mi + 2 * MLSTM_HEADS
    o_ff = o_fq + 3 * FOX_W
    o_ga = o_ff + FOX_HEADS
    w_gate_cols = jnp.concatenate([w_in[:, o_mi:o_fq], w_in[:, o_ff:o_ga]], axis=1)
    w_gate_cols = jnp.pad(w_gate_cols, ((0, 0), (0, LANES - N_GATES)))
    w_streams = bf(jnp.concatenate([w_in[:, :o_mo], w_in[:, o_fq:o_ff], w_gate_cols], axis=1))
    w_late = bf(jnp.concatenate([w_in[:, o_mo:o_mi], w_in[:, o_ga:]], axis=1))
    gate_bias = jnp.pad(jnp.concatenate([mlstm_i_bias, mlstm_f_bias, fox_f_bias]), (0, LANES - N_GATES))
    consts = _routing_constants()

    ffn1 = (vec(ffn1_pre_g), bf(ffn1_w_gate), bf(ffn1_w_up), bf(ffn1_w_down), vec(ffn1_post_g))
    h1, mqk, mv, fq, fk, fvt, zs = _ffn_in_proj(h, ffn1, vec(mix_pre_g), w_streams, batch=batch, seq=seq)
    per_seq = lambda a: a.reshape(batch, seq, a.shape[-1])
    ga, gb, kaug = _gates(per_seq(zs), vec(gate_bias), per_seq(fk), consts)
    ya = _mlstm(per_seq(mqk), per_seq(mv), ga, gb, consts[-1], conv_w, vec(conv_b), vec(mlstm_norm_g))
    yb = _fox(per_seq(fq), kaug, fvt)

    h2 = _merge(h1, ya.reshape(batch * seq, MLSTM_V), yb.reshape(batch * seq, FOX_W), vec(mix_pre_g), w_late,
                vec(branch_gate_bias), bf(w_branch_a), bf(w_branch_b), bf(w_out), vec(mix_post_g))
    ffn2 = (vec(ffn2_pre_g), bf(ffn2_w_gate), bf(ffn2_w_up), bf(ffn2_w_down), vec(ffn2_post_g))
    ple = (vec(ple_pre_g), bf(ple_w_gate), vec(ple_b_gate), bf(ple_w_proj), vec(ple_post_g))
    return _ffn_ple(h2, ffn2, p, ple)


def kernel(x, p, ffn1_pre_g, ffn1_w_gate, ffn1_w_up, ffn1_w_down, ffn1_post_g, mix_pre_g, w_in, conv_w, conv_b, mlstm_i_bias, mlstm_f_bias, mlstm_norm_g, fox_f_bias, branch_gate_bias, w_branch_a, w_branch_b, w_out, mix_post_g, ffn2_pre_g, ffn2_w_gate, ffn2_w_up, ffn2_w_down, ffn2_post_g, ple_pre_g, ple_w_gate, ple_b_gate, ple_w_proj, ple_post_g):
    batch, seq, d = x.shape
    weights = (ffn1_pre_g, ffn1_w_gate, ffn1_w_up, ffn1_w_down, ffn1_post_g, mix_pre_g, w_in, conv_w, conv_b,
               mlstm_i_bias, mlstm_f_bias, mlstm_norm_g, fox_f_bias, branch_gate_bias, w_branch_a, w_branch_b,
               w_out, mix_post_g, ffn2_pre_g, ffn2_w_gate, ffn2_w_up, ffn2_w_down, ffn2_post_g, ple_pre_g,
               ple_w_gate, ple_b_gate, ple_w_proj, ple_post_g)
    h = x.reshape(batch * seq, d)
    for layer in range(p.shape[0]):
        h = _layer(h, p[layer].reshape(batch * seq, -1), *(w[layer] for w in weights), batch=batch, seq=seq)
    return h.reshape(batch, seq, d)
```

```python
import math

import jax
import jax.numpy as jnp
import numpy as np
from jax import lax
from jax.experimental import pallas as pl
from jax.experimental.pallas import tpu as pltpu

F32 = jnp.float32
BF16 = jnp.bfloat16

EPS = 1e-6
MLSTM_HEADS = 4
MLSTM_DQK = 64
MLSTM_DV = 128
MLSTM_QK = MLSTM_HEADS * MLSTM_DQK
MLSTM_V = MLSTM_HEADS * MLSTM_DV
CONV_WIDTH = 4
FOX_HEADS = 8
FOX_DH = 64
FOX_W = FOX_HEADS * FOX_DH
N_GATES = 2 * MLSTM_HEADS + FOX_HEADS
LOG2E = math.log2(math.e)

LANES = 128
SUBLANES = 8
VMEM_LIMIT_BYTES = 56 * 1024 * 1024

TOKEN_TILE = 1024
FFN_ROW_SPLIT = 2
FFN_CHUNK = 256
GATE_TILE = 1024
MLSTM_CHUNK = LANES
MLSTM_TILE = 512
MLSTM_SLOTS = 16
N_SPLIT = 3
FOX_TQ = 512
FOX_TK = 4 * FOX_TQ
FOX_SUB = 1024
FOX_AUG_STRIDE = 8


def _params(*semantics, flags=None):
    return pltpu.CompilerParams(dimension_semantics=semantics, vmem_limit_bytes=VMEM_LIMIT_BYTES, flags=flags)


def _resident(shape):
    nd = len(shape)
    return pl.BlockSpec(shape, lambda *_: (0,) * nd, pipeline_mode=pl.Buffered(1))


def _rms(x, g):
    return x * lax.rsqrt(jnp.mean(x * x, axis=-1, keepdims=True) + EPS) * g


def _dot(a, b):
    return jnp.dot(a, b, preferred_element_type=F32)


def _dot_nt(a, b):
    return lax.dot_general(a, b, (((1,), (1,)), ((), ())), preferred_element_type=F32)


def _dot_tn(a, b):
    return lax.dot_general(a, b, (((0,), (0,)), ((), ())), preferred_element_type=F32)


def _ffn_half_step(x, pre_g_ref, wg_ref, wu_ref, wd_ref, post_g_ref):
    d_ff = wg_ref.shape[1]
    rows = x.shape[0] // FFN_ROW_SPLIT
    xs = [x[r * rows:(r + 1) * rows, :] for r in range(FFN_ROW_SPLIT)]
    xn = [_rms(xr, pre_g_ref[...]).astype(BF16) for xr in xs]
    acc = [jnp.zeros(xr.shape, F32) for xr in xs]
    for c in range(d_ff // FFN_CHUNK):
        sl = slice(c * FFN_CHUNK, (c + 1) * FFN_CHUNK)
        for r in range(FFN_ROW_SPLIT):
            g = _dot(xn[r], wg_ref[:, sl])
            u = _dot(xn[r], wu_ref[:, sl])
            a = (g * jax.nn.sigmoid(g) * u).astype(BF16)
            acc[r] = acc[r] + _dot(a, wd_ref[sl, :])
    return [xr + 0.5 * _rms(ar, post_g_ref[...]) for xr, ar in zip(xs, acc)]


def _ffn_in_proj_kernel(x_ref, pre_g_ref, wg_ref, wu_ref, wd_ref, post_g_ref, mix_g_ref, w_ref,
                        h_ref, mqk_ref, mv_ref, fq_ref, fk_ref, fvt_ref, zs_ref):
    hs = _ffn_half_step(x_ref[...], pre_g_ref, wg_ref, wu_ref, wd_ref, post_g_ref)
    rows = x_ref.shape[0] // FFN_ROW_SPLIT
    for r, h in enumerate(hs):
        rs = slice(r * rows, (r + 1) * rows)
        h_ref[rs, :] = h
        u = _rms(h, mix_g_ref[...]).astype(BF16)
        z = _dot(u, w_ref[...])
        o = 2 * MLSTM_QK
        mqk_ref[rs, :] = z[:, :o].astype(BF16)
        mv_ref[rs, :] = z[:, o:o + MLSTM_V].astype(BF16)
        o += MLSTM_V
        fq_ref[rs, :] = (z[:, o:o + FOX_W] * (FOX_DH ** -0.5 * LOG2E)).astype(BF16)
        fk_ref[rs, :] = z[:, o + FOX_W:o + 2 * FOX_W].astype(BF16)
        fvt_ref[0, :, rs] = z[:, o + 2 * FOX_W:o + 3 * FOX_W].T.astype(BF16)
        zs_ref[rs, :] = z[:, o + 3 * FOX_W:]


def _ffn_in_proj(x, ffn, mix_g, w, *, batch, seq):
    t, d = x.shape
    pre_g, w_gate, w_up, w_down, post_g = ffn
    d_ff = w_gate.shape[1]
    n = w.shape[1]
    tiles = seq // TOKEN_TILE
    assert seq % TOKEN_TILE == 0 and d_ff % FFN_CHUNK == 0
    row = lambda width: pl.BlockSpec((TOKEN_TILE, width), lambda i: (i, 0))
    widths = (2 * MLSTM_QK, MLSTM_V, FOX_W, FOX_W)
    return pl.pallas_call(
        _ffn_in_proj_kernel,
        grid=(t // TOKEN_TILE,),
        in_specs=[row(d), _resident((1, d)), _resident((d, d_ff)), _resident((d, d_ff)),
                  _resident((d_ff, d)), _resident((1, d)), _resident((1, d)), _resident((d, n))],
        out_specs=[row(d)] + [row(wd) for wd in widths]
        + [pl.BlockSpec((1, FOX_W, TOKEN_TILE), lambda i: (i // tiles, 0, i % tiles)), row(LANES)],
        out_shape=[jax.ShapeDtypeStruct((t, d), F32)] + [jax.ShapeDtypeStruct((t, wd), BF16) for wd in widths]
        + [jax.ShapeDtypeStruct((batch, FOX_W, seq), BF16), jax.ShapeDtypeStruct((t, LANES), F32)],
        compiler_params=_params("parallel"),
        name="ffn_in_proj",
    )(x, pre_g, w_gate, w_up, w_down, post_g, mix_g, w)


def _ffn_ple_kernel(x_ref, pre_g_ref, wg_ref, wu_ref, wd_ref, post_g_ref,
                    p_ref, ple_pre_g_ref, ple_wg_ref, ple_bg_ref, ple_wp_ref, ple_post_g_ref, o_ref):
    hs = _ffn_half_step(x_ref[...], pre_g_ref, wg_ref, wu_ref, wd_ref, post_g_ref)
    rows = x_ref.shape[0] // FFN_ROW_SPLIT
    for r, h in enumerate(hs):
        rs = slice(r * rows, (r + 1) * rows)
        u = _rms(h, ple_pre_g_ref[...]).astype(BF16)
        gate = jax.nn.sigmoid(_dot(u, ple_wg_ref[...]) + ple_bg_ref[...])
        emb = _dot(p_ref[rs, :].astype(BF16), ple_wp_ref[...])
        o_ref[rs, :] = h + _rms(gate * emb, ple_post_g_ref[...])


def _ffn_ple(x, ffn, p, ple):
    t, d = x.shape
    pre_g, w_gate, w_up, w_down, post_g = ffn
    ple_pre_g, ple_w_g, ple_b_g, ple_w_p, ple_post_g = ple
    d_ff = w_gate.shape[1]
    dp = p.shape[1]
    assert t % TOKEN_TILE == 0 and d_ff % FFN_CHUNK == 0
    row = lambda width: pl.BlockSpec((TOKEN_TILE, width), lambda i: (i, 0))
    return pl.pallas_call(
        _ffn_ple_kernel,
        grid=(t // TOKEN_TILE,),
        in_specs=[row(d), _resident((1, d)), _resident((d, d_ff)), _resident((d, d_ff)),
                  _resident((d_ff, d)), _resident((1, d)),
                  row(dp), _resident((1, d)), _resident((d, d)), _resident((1, d)), _resident((dp, d)),
                  _resident((1, d))],
        out_specs=row(d),
        out_shape=jax.ShapeDtypeStruct((t, d), F32),
        compiler_params=_params("parallel"),
        name="ffn_ple",
    )(x, pre_g, w_gate, w_up, w_down, post_g, p, ple_pre_g, ple_w_g, ple_b_g, ple_w_p, ple_post_g)


def _split3(x):
    hi = x.astype(BF16)
    r1 = x - hi.astype(F32)
    mid = r1.astype(BF16)
    lo = (r1 - mid.astype(F32)).astype(BF16)
    return hi, mid, lo


def _fox_aug_lane(hh, pair):
    return (FOX_DH if hh == 0 else 0) + FOX_AUG_STRIDE * pair


def _routing_constants():
    h_, sl = MLSTM_HEADS, MLSTM_SLOTS
    ra = np.zeros((N_SPLIT * LANES, LANES), np.float32)
    rb = np.zeros((N_SPLIT * LANES, LANES), np.float32)
    ones_a = np.zeros((1, LANES), np.float32)
    ones_b = np.zeros((1, LANES), np.float32)
    bconst = np.zeros((h_, LANES, 2 * LANES), np.float32)
    for h in range(h_):
        for j in range(N_SPLIT):
            ra[j * LANES + h_ + h, sl * h + j] = 1.0
            ra[j * LANES + h, sl * h + N_SPLIT + j] = 1.0
            rb[j * LANES + h, sl * h + 2 * N_SPLIT + j] = 1.0
            rb[j * LANES + h_ + h, sl * h + 3 * N_SPLIT + j] = -1.0
            bconst[h, sl * h + j, :LANES] = 1.0
            bconst[h, sl * h + j, LANES:] = -1.0
            bconst[h, sl * h + N_SPLIT + j, LANES:] = 1.0
        ones_a[0, sl * h + 2 * N_SPLIT:sl * h + 4 * N_SPLIT] = 1.0
        ones_b[0, sl * h:sl * h + N_SPLIT] = 1.0
    rk = np.zeros((N_SPLIT * LANES, LANES), np.float32)
    for h in range(FOX_HEADS):
        for j in range(N_SPLIT):
            rk[j * LANES + 2 * h_ + h, _fox_aug_lane(h % 2, h // 2) + j] = -1.0
    as_bf = lambda a: jnp.asarray(a, BF16)
    return as_bf(ra), jnp.asarray(ones_a), as_bf(rb), jnp.asarray(ones_b), as_bf(rk), as_bf(bconst)


def _gates_kernel(zs_ref, bias_ref, fk_ref, ra_ref, ones_a_ref, rb_ref, ones_b_ref, rk_ref,
                  ga_ref, gb_ref, kaug_ref, carry_ref):
    @pl.when(pl.program_id(1) == 0)
    def _():
        carry_ref[...] = jnp.zeros_like(carry_ref)

    x = zs_ref[0] + bias_ref[...]
    n = x.shape[0]
    log_f = jnp.minimum(x, 0.0) - jnp.log1p(jnp.exp(-jnp.abs(x)))
    L = MLSTM_CHUNK
    tri = jnp.where(lax.broadcasted_iota(jnp.int32, (L, L), 0) >= lax.broadcasted_iota(jnp.int32, (L, L), 1),
                    1.0, 0.0).astype(BF16)
    parts = _split3(log_f)
    cum_chunk = jnp.concatenate(
        [sum(_dot(tri, part[ci * L:(ci + 1) * L, :]) for part in parts) for ci in range(n // L)], axis=0)
    row = lax.broadcasted_iota(jnp.int32, x.shape, 0)
    before = carry_ref[0:1, :]
    offset = jnp.broadcast_to(before, x.shape)
    for ci in range(1, n // MLSTM_CHUNK):
        before = before + cum_chunk[ci * MLSTM_CHUNK - 1:ci * MLSTM_CHUNK, :]
        offset = jnp.where(row >= ci * MLSTM_CHUNK, before, offset)
    cum_all = cum_chunk + offset
    carry_ref[0:1, :] = cum_all[n - 1:n, :]
    col = lax.broadcasted_iota(jnp.int32, x.shape, 1)
    vals = jnp.where(col < MLSTM_HEADS, x, jnp.where(col < 2 * MLSTM_HEADS, cum_chunk, cum_all * LOG2E))

    split = jnp.concatenate(_split3(vals), axis=1)
    ga_ref[0] = (_dot(split, ra_ref[...]) + ones_a_ref[...]).astype(BF16)
    gb_ref[0] = (_dot(split, rb_ref[...]) + ones_b_ref[...]).T.astype(BF16)
    aug = _dot(split, rk_ref[...])
    lane = lax.broadcasted_iota(jnp.int32, (1, LANES), 1)
    lo_half = lane < FOX_DH
    for h in range(FOX_HEADS):
        a0 = _fox_aug_lane(h % 2, h // 2)
        mine = jnp.where(lane >= a0, jnp.where(lane < a0 + N_SPLIT, aug, 0.0), 0.0).astype(BF16)
        k_pair = fk_ref[0, :, (h // 2) * LANES:(h // 2 + 1) * LANES]
        own = lo_half if h % 2 == 0 else jnp.logical_not(lo_half)
        kaug_ref[0, :, h * LANES:(h + 1) * LANES] = jnp.where(own, k_pair, mine)


def _gates(zs, bias, fk, consts):
    b, s, _ = zs.shape
    ra, ones_a, rb, ones_b, rk, _ = consts
    assert s % GATE_TILE == 0 and GATE_TILE % MLSTM_CHUNK == 0
    tile = lambda width: pl.BlockSpec((1, GATE_TILE, width), lambda i, j: (i, j, 0))
    return pl.pallas_call(
        _gates_kernel,
        grid=(b, s // GATE_TILE),
        in_specs=[tile(LANES), pl.BlockSpec((1, LANES), lambda i, j: (0, 0)), tile(FOX_W),
                  _resident(ra.shape), _resident(ones_a.shape), _resident(rb.shape), _resident(ones_b.shape),
                  _resident(rk.shape)],
        out_specs=[tile(LANES), pl.BlockSpec((1, LANES, GATE_TILE), lambda i, j: (i, 0, j)),
                   tile(FOX_HEADS * LANES)],
        out_shape=[jax.ShapeDtypeStruct((b, s, LANES), BF16),
                   jax.ShapeDtypeStruct((b, LANES, s), BF16),
                   jax.ShapeDtypeStruct((b, s, FOX_HEADS * LANES), BF16)],
        scratch_shapes=[pltpu.VMEM((SUBLANES, LANES), F32)],
        compiler_params=_params("parallel", "arbitrary"),
        name="gates",
    )(zs, bias, fk, ra, ones_a, rb, ones_b, rk)


def _mlstm_kernel(qk_ref, v_ref, ga_ref, gb_ref, bconst_ref, cw_ref, cb_ref, ng_ref, o_ref,
                  tail_ref, c_ref, m_ref):
    @pl.when(pl.program_id(1) == 0)
    def _():
        tail_ref[...] = jnp.zeros_like(tail_ref)
        c_ref[...] = jnp.zeros_like(c_ref)
        m_ref[...] = jnp.zeros_like(m_ref)

    L = MLSTM_CHUNK
    rows = qk_ref.shape[1]

    cur = qk_ref[0].astype(F32)
    ext = jnp.concatenate([tail_ref[...], cur], axis=0)
    conv = cb_ref[...] + cw_ref[CONV_WIDTH - 1:CONV_WIDTH, :] * cur
    for d in range(1, CONV_WIDTH):
        shifted = pltpu.roll(ext, d, 0)[SUBLANES:, :]
        conv = conv + cw_ref[CONV_WIDTH - 1 - d:CONV_WIDTH - d, :] * shifted
    tail_ref[...] = cur[rows - SUBLANES:, :]
    qk = conv * jax.nn.sigmoid(conv)
    q_all = qk[:, :MLSTM_QK]
    k_all = qk[:, MLSTM_QK:] * (MLSTM_DQK ** -0.5)

    lane = lax.broadcasted_iota(jnp.int32, (1, LANES), 1)
    lo_half = lane < MLSTM_DQK
    srow = lax.broadcasted_iota(jnp.int32, (LANES, 1), 0) < MLSTM_DQK
    slot_owner = lax.broadcasted_iota(jnp.int32, (LANES, 1), 0) // MLSTM_SLOTS
    tril = (lax.broadcasted_iota(jnp.int32, (L, L), 0) >= lax.broadcasted_iota(jnp.int32, (L, L), 1))
    ones_block = jnp.ones((L, LANES), BF16)
    twice = lambda a: jnp.concatenate([a, a], axis=1)

    chunks = range(rows // L)
    heads = range(MLSTM_HEADS)
    units = [(ci, h) for ci in chunks for h in heads]
    rs = {ci: slice(ci * L, (ci + 1) * L) for ci in chunks}

    b_rep, log_d, m_intra, log_w, w_max, b_last = {}, {}, {}, {}, {}, {}
    for ci, h in units:
        key_side = jnp.where(slot_owner == h, gb_ref[0, :, rs[ci]], jnp.zeros((), BF16))
        g = _dot(ga_ref[0, rs[ci], :], jnp.concatenate([key_side, bconst_ref[h]], axis=1))
        log_d[ci, h] = jnp.where(tril, g[:, :L], -jnp.inf)
        b_rep[ci, h] = g[:, L:L + LANES]
        b_last[ci, h] = b_rep[ci, h][L - 1:L, :]
        m_intra[ci, h] = jnp.max(log_d[ci, h], axis=-1, keepdims=True)
        log_w[ci, h] = b_last[ci, h] + g[:, L + LANES:]
        w_max[ci, h] = jnp.max(log_w[ci, h], axis=0, keepdims=True)

    m_st, m_next, decay = {}, {}, {}
    for h in heads:
        m = m_ref[h:h + 1, :]
        for ci in chunks:
            m_st[ci, h] = m
            m = jnp.maximum(b_last[ci, h] + m, w_max[ci, h])
            m_next[ci, h] = m
            decay[ci, h] = twice(jnp.exp(b_last[ci, h] + m_st[ci, h] - m))
        m_ref[h:h + 1, :] = m

    q_h, k_pair, k_pair_b, v_aug = {}, {}, {}, {}
    for ci in chunks:
        for p in range(MLSTM_HEADS // 2):
            ps = slice(p * LANES, (p + 1) * LANES)
            k_pair[ci, p] = k_all[rs[ci], ps]
            k_pair_b[ci, p] = k_pair[ci, p].astype(BF16)
            q_pair = q_all[rs[ci], ps]
            q_h[ci, 2 * p] = jnp.where(lo_half, q_pair, 0.0).astype(BF16)
            q_h[ci, 2 * p + 1] = jnp.where(lo_half, 0.0, q_pair).astype(BF16)
        for h in heads:
            v_aug[ci, h] = jnp.concatenate([v_ref[0, rs[ci], h * MLSTM_DV:(h + 1) * MLSTM_DV], ones_block], axis=1)

    scores = {u: _dot_nt(q_h[u], k_pair_b[u[0], u[1] // 2]) for u in units}
    upd = {}
    for ci, h in units:
        w = jnp.exp(log_w[ci, h] - m_next[ci, h])
        upd[ci, h] = _dot_tn((k_pair[ci, h // 2] * w).astype(BF16), v_aug[ci, h])

    c_b = {}
    for p in range(MLSTM_HEADS // 2):
        c_pair = c_ref[p]
        for ci in chunks:
            c_b[ci, p] = c_pair.astype(BF16)
            c_pair = jnp.where(srow, decay[ci, 2 * p] * c_pair + upd[ci, 2 * p],
                               decay[ci, 2 * p + 1] * c_pair + upd[ci, 2 * p + 1])
        c_ref[p] = c_pair

    m_t, nd = {}, {}
    for ci, h in units:
        m_t[ci, h] = jnp.maximum(b_rep[ci, h] + m_st[ci, h], m_intra[ci, h])
        sc = (scores[ci, h] * jnp.exp(log_d[ci, h] - m_t[ci, h])).astype(BF16)
        inter = jnp.exp(b_rep[ci, h] + m_st[ci, h] - m_t[ci, h])
        nd[ci, h] = _dot(sc, v_aug[ci, h]) + twice(inter) * _dot(q_h[ci, h], c_b[ci, h // 2])

    hm = {}
    for u in units:
        den = jnp.maximum(jnp.abs(nd[u][:, MLSTM_DV:]), jnp.exp(-m_t[u]))
        hm[u] = nd[u][:, :MLSTM_DV] / den
    cen = {u: hm[u] - jnp.mean(hm[u], axis=-1, keepdims=True) for u in units}
    var = {u: jnp.mean(cen[u] * cen[u], axis=-1, keepdims=True) for u in units}
    for ci, h in units:
        cols = slice(h * MLSTM_DV, (h + 1) * MLSTM_DV)
        o_ref[0, rs[ci], cols] = (cen[ci, h] * lax.rsqrt(var[ci, h] + EPS) * ng_ref[:, cols]).astype(o_ref.dtype)


def _mlstm(mqk, mv, ga, gb, bconst, conv_w, conv_b, norm_g):
    b, s, _ = mqk.shape
    assert s % MLSTM_TILE == 0 and MLSTM_TILE % MLSTM_CHUNK == 0
    assert MLSTM_DV == LANES and MLSTM_HEADS * MLSTM_SLOTS <= LANES
    slab = lambda width: pl.BlockSpec((1, MLSTM_TILE, width), lambda i, j: (i, j, 0))
    return pl.pallas_call(
        _mlstm_kernel,
        grid=(b, s // MLSTM_TILE),
        in_specs=[slab(2 * MLSTM_QK), slab(MLSTM_V), slab(LANES),
                  pl.BlockSpec((1, LANES, MLSTM_TILE), lambda i, j: (i, 0, j)),
                  _resident(bconst.shape),
                  _resident((CONV_WIDTH, 2 * MLSTM_QK)), _resident((1, 2 * MLSTM_QK)),
                  _resident((1, MLSTM_V))],
        out_specs=slab(MLSTM_V),
        out_shape=jax.ShapeDtypeStruct((b, s, MLSTM_V), BF16),
        scratch_shapes=[pltpu.VMEM((SUBLANES, 2 * MLSTM_QK), F32),
                        pltpu.VMEM((MLSTM_HEADS // 2, LANES, 2 * LANES), F32),
                        pltpu.VMEM((SUBLANES, LANES), F32)],
        compiler_params=_params("parallel", "arbitrary"),
        name="mlstm",
    )(mqk, mv, ga, gb, bconst, conv_w, conv_b, norm_g)


def _fox_kernel(q_ref, k_ref, vt_ref, o_ref):
    tq = q_ref.shape[1]
    pair = pl.program_id(1)
    i = pl.program_id(2)
    q = q_ref[0]
    lane = lax.broadcasted_iota(jnp.int32, (1, LANES), 1)
    lo_half = lane < FOX_DH
    q_aug = []
    for hh in range(2):
        a0 = _fox_aug_lane(hh, pair)
        bias_lanes = jnp.where(lane >= a0, jnp.where(lane < a0 + N_SPLIT, 1.0, 0.0), 0.0).astype(BF16)
        q_aug.append(jnp.where(lo_half if hh == 0 else jnp.logical_not(lo_half), q, bias_lanes))

    def tile(start, width, state, masked):
        pieces = [(off, min(FOX_SUB, width - off)) for off in range(0, width, FOX_SUB)]
        keys = [pl.ds(pl.multiple_of(start + off, FOX_TQ), w) for off, w in pieces]
        scores = [[_dot_nt(k_ref[0, ks, hh * LANES:(hh + 1) * LANES], q_aug[hh]) for hh in range(2)] for ks in keys]
        state = list(state)
        for (off, w), ks, s_pair in zip(pieces, keys, scores):
            for hh in range(2):
                m, l, acc = state[hh]
                s = s_pair[hh]
                if masked:
                    key_pos = start + off + lax.broadcasted_iota(jnp.int32, (w, tq), 0)
                    query_pos = i * tq + lax.broadcasted_iota(jnp.int32, (w, tq), 1)
                    s = jnp.where(key_pos <= query_pos, s, -jnp.inf)
                m_new = jnp.maximum(m, jnp.max(s, axis=0, keepdims=True))
                alpha = jnp.exp2(m - m_new)
                p = jnp.exp2(s - m_new)
                l = alpha * l + jnp.sum(p, axis=0, keepdims=True)
                acc = alpha * acc + _dot(vt_ref[0, hh * FOX_DH:(hh + 1) * FOX_DH, ks], p.astype(BF16))
                state[hh] = (m_new, l, acc)
        return tuple(state)

    state = tuple((jnp.full((1, tq), -jnp.inf, F32), jnp.zeros((1, tq), F32), jnp.zeros((FOX_DH, tq), F32))
                  for _ in range(2))
    n_full = (i * tq) // FOX_TK
    state = lax.fori_loop(0, n_full, lambda j, st: tile(j * FOX_TK, FOX_TK, st, False), state)
    rest = n_full * FOX_TK
    tails = [(lambda st, w=w: tile(rest, w, st, True)) for w in range(tq, FOX_TK + tq, tq)]
    (_, l0, a0), (_, l1, a1) = lax.switch((i * tq - rest) // tq, tails, state)
    o_ref[0] = jnp.concatenate([a0 / l0, a1 / l1], axis=0).T.astype(o_ref.dtype)


def _fox(fq, kaug, fvt):
    b, s, _ = fq.shape
    assert FOX_TK % FOX_TQ == 0 and s % FOX_TK == 0 and 2 * FOX_DH == LANES
    pairs = FOX_HEADS // 2
    return pl.pallas_call(
        _fox_kernel,
        grid=(b, pairs, s // FOX_TQ),
        in_specs=[pl.BlockSpec((1, FOX_TQ, LANES), lambda bi, p, i: (bi, i, p)),
                  pl.BlockSpec((1, s, 2 * LANES), lambda bi, p, i: (bi, 0, p)),
                  pl.BlockSpec((1, LANES, s), lambda bi, p, i: (bi, p, 0))],
        out_specs=pl.BlockSpec((1, FOX_TQ, LANES), lambda bi, p, i: (bi, i, p)),
        out_shape=jax.ShapeDtypeStruct((b, s, FOX_W), BF16),
        compiler_params=_params("parallel", "parallel", "arbitrary"),
        name="fox",
    )(fq, kaug, fvt)


def _merge_kernel(h_ref, ya_ref, yb_ref, pre_g_ref, wu_ref, bias_ref, wa_ref, wb_ref, wo_ref, post_g_ref, o_ref):
    h = h_ref[...]
    d = h.shape[1]
    u = _rms(h, pre_g_ref[...]).astype(BF16)
    zu = _dot(u, wu_ref[...])
    y_a = (jax.nn.sigmoid(zu[:, :MLSTM_V]) * ya_ref[...].astype(F32)).astype(BF16)
    gates = jax.nn.sigmoid(zu[:, MLSTM_V:] + bias_ref[...])
    merged = gates[:, :d] * _dot(y_a, wa_ref[...]) + gates[:, d:] * _dot(yb_ref[...], wb_ref[...])
    o_ref[...] = h + _rms(_dot(merged.astype(BF16), wo_ref[...]), post_g_ref[...])


def _merge(h, ya, yb, pre_g, w_u, bias, w_a, w_b, w_o, post_g):
    t, d = h.shape
    row = lambda width: pl.BlockSpec((TOKEN_TILE, width), lambda i: (i, 0))
    return pl.pallas_call(
        _merge_kernel,
        grid=(t // TOKEN_TILE,),
        in_specs=[row(d), row(MLSTM_V), row(FOX_W), _resident((1, d)), _resident(w_u.shape),
                  _resident((1, 2 * d)), _resident(w_a.shape), _resident(w_b.shape), _resident((d, d)),
                  _resident((1, d))],
        out_specs=row(d),
        out_shape=jax.ShapeDtypeStruct((t, d), F32),
        compiler_params=_params("parallel"),
        name="merge",
    )(h, ya, yb, pre_g, w_u, bias, w_a, w_b, w_o, post_g)


def _layer(h, p, ffn1_pre_g, ffn1_w_gate, ffn1_w_up, ffn1_w_down, ffn1_post_g,
           mix_pre_g, w_in, conv_w, conv_b, mlstm_i_bias, mlstm_f_bias, mlstm_norm_g,
           fox_f_bias, branch_gate_bias, w_branch_a, w_branch_b, w_out, mix_post_g,
           ffn2_pre_g, ffn2_w_gate, ffn2_w_up, ffn2_w_down, ffn2_post_g,
           ple_pre_g, ple_w_gate, ple_b_gate, ple_w_proj, ple_post_g, *, batch, seq):
    vec = lambda a: a.reshape(1, -1)
    bf = lambda a: a.astype(BF16)

    o_mo = 2 * MLSTM_QK + MLSTM_V
    o_mi = o_mo + MLSTM_V
    o_fq = o_mi + 2 * MLSTM_HEADS
    o_ff = o_fq + 3 * FOX_W
    o_ga = o_ff + FOX_HEADS
    w_gate_cols = jnp.concatenate([w_in[:, o_mi:o_fq], w_in[:, o_ff:o_ga]], axis=1)
    w_gate_cols = jnp.pad(w_gate_cols, ((0, 0), (0, LANES - N_GATES)))
    w_streams = bf(jnp.concatenate([w_in[:, :o_mo], w_in[:, o_fq:o_ff], w_gate_cols], axis=1))
    w_late = bf(jnp.concatenate([w_in[:, o_mo:o_mi], w_in[:, o_ga:]], axis=1))
    gate_bias = jnp.pad(jnp.concatenate([mlstm_i_bias, mlstm_f_bias, fox_f_bias]), (0, LANES - N_GATES))
    consts = _routing_constants()

    ffn1 = (vec(ffn1_pre_g), bf(ffn1_w_gate), bf(ffn1_w_up), bf(ffn1_w_down), vec(ffn1_post_g))
    h1, mqk, mv, fq, fk, fvt, zs = _ffn_in_proj(h, ffn1, vec(mix_pre_g), w_streams, batch=batch, seq=seq)
    per_seq = lambda a: a.reshape(batch, seq, a.shape[-1])
    ga, gb, kaug = _gates(per_seq(zs), vec(gate_bias), per_seq(fk), consts)
    ya = _mlstm(per_seq(mqk), per_seq(mv), ga, gb, consts[-1], conv_w, vec(conv_b), vec(mlstm_norm_g))
    yb = _fox(per_seq(fq), kaug, fvt)

    h2 = _merge(h1, ya.reshape(batch * seq, MLSTM_V), yb.reshape(batch * seq, FOX_W), vec(mix_pre_g), w_late,
                vec(branch_gate_bias), bf(w_branch_a), bf(w_branch_b), bf(w_out), vec(mix_post_g))
    ffn2 = (vec(ffn2_pre_g), bf(ffn2_w_gate), bf(ffn2_w_up), bf(ffn2_w_down), vec(ffn2_post_g))
    ple = (vec(ple_pre_g), bf(ple_w_gate), vec(ple_b_gate), bf(ple_w_proj), vec(ple_post_g))
    return _ffn_ple(h2, ffn2, p, ple)


def kernel(x, p, ffn1_pre_g, ffn1_w_gate, ffn1_w_up, ffn1_w_down, ffn1_post_g, mix_pre_g, w_in, conv_w, conv_b, mlstm_i_bias, mlstm_f_bias, mlstm_norm_g, fox_f_bias, branch_gate_bias, w_branch_a, w_branch_b, w_out, mix_post_g, ffn2_pre_g, ffn2_w_gate, ffn2_w_up, ffn2_w_down, ffn2_post_g, ple_pre_g, ple_w_gate, ple_b_gate, ple_w_proj, ple_post_g):
    batch, seq, d = x.shape
    weights = (ffn1_pre_g, ffn1_w_gate, ffn1_w_up, ffn1_w_down, ffn1_post_g, mix_pre_g, w_in, conv_w, conv_b,
               mlstm_i_bias, mlstm_f_bias, mlstm_norm_g, fox_f_bias, branch_gate_bias, w_branch_a, w_branch_b,
               w_out, mix_post_g, ffn2_pre_g, ffn2_w_gate, ffn2_w_up, ffn2_w_down, ffn2_post_g, ple_pre_g,
               ple_w_gate, ple_b_gate, ple_w_proj, ple_post_g)
    h = x.reshape(batch * seq, d)
    for layer in range(p.shape[0]):
        h = _layer(h, p[layer].reshape(batch * seq, -1), *(w[layer] for w in weights), batch=batch, seq=seq)
    return h.reshape(batch, seq, d)
```

```python
import math

import jax
import jax.numpy as jnp
import numpy as np
from jax import lax
from jax.experimental import pallas as pl
from jax.experimental.pallas import tpu as pltpu

F32 = jnp.float32
BF16 = jnp.bfloat16

EPS = 1e-6
MLSTM_HEADS = 4
MLSTM_DQK = 64
MLSTM_DV = 128
MLSTM_QK = MLSTM_HEADS * MLSTM_DQK
MLSTM_V = MLSTM_HEADS * MLSTM_DV
CONV_WIDTH = 4
FOX_HEADS = 8
FOX_DH = 64
FOX_W = FOX_HEADS * FOX_DH
N_GATES = 2 * MLSTM_HEADS + FOX_HEADS
LOG2E = math.log2(math.e)

LANES = 128
SUBLANES = 8
VMEM_LIMIT_BYTES = 56 * 1024 * 1024

TOKEN_TILE = 1024
FFN_ROW_SPLIT = 2
FFN_CHUNK = 256
GATE_TILE = 1024
MLSTM_CHUNK = LANES
MLSTM_TILE = 512
MLSTM_SLOTS = 16
N_SPLIT = 3
FOX_TQ = 512
FOX_SUB = 1024
FOX_AUG_STRIDE = 8


def _params(*semantics, flags=None):
    return pltpu.CompilerParams(dimension_semantics=semantics, vmem_limit_bytes=VMEM_LIMIT_BYTES, flags=flags)


def _resident(shape):
    nd = len(shape)
    return pl.BlockSpec(shape, lambda *_: (0,) * nd, pipeline_mode=pl.Buffered(1))


def _rms(x, g):
    return x * lax.rsqrt(jnp.mean(x * x, axis=-1, keepdims=True) + EPS) * g


def _dot(a, b):
    return jnp.dot(a, b, preferred_element_type=F32)


def _dot_nt(a, b):
    return lax.dot_general(a, b, (((1,), (1,)), ((), ())), preferred_element_type=F32)


def _dot_tn(a, b):
    return lax.dot_general(a, b, (((0,), (0,)), ((), ())), preferred_element_type=F32)


def _ffn_half_step(x, pre_g_ref, wg_ref, wu_ref, wd_ref, post_g_ref):
    d_ff = wg_ref.shape[1]
    rows = x.shape[0] // FFN_ROW_SPLIT
    xs = [x[r * rows:(r + 1) * rows, :] for r in range(FFN_ROW_SPLIT)]
    xn = [_rms(xr, pre_g_ref[...]).astype(BF16) for xr in xs]
    acc = [jnp.zeros(xr.shape, F32) for xr in xs]
    for c in range(d_ff // FFN_CHUNK):
        sl = slice(c * FFN_CHUNK, (c + 1) * FFN_CHUNK)
        for r in range(FFN_ROW_SPLIT):
            g = _dot(xn[r], wg_ref[:, sl])
            u = _dot(xn[r], wu_ref[:, sl])
            a = (g * jax.nn.sigmoid(g) * u).astype(BF16)
            acc[r] = acc[r] + _dot(a, wd_ref[sl, :])
    return [xr + 0.5 * _rms(ar, post_g_ref[...]) for xr, ar in zip(xs, acc)]


def _ffn_in_proj_kernel(x_ref, pre_g_ref, wg_ref, wu_ref, wd_ref, post_g_ref, mix_g_ref, w_ref,
                        h_ref, mqk_ref, mv_ref, fq_ref, fk_ref, fvt_ref, zs_ref):
    hs = _ffn_half_step(x_ref[...], pre_g_ref, wg_ref, wu_ref, wd_ref, post_g_ref)
    rows = x_ref.shape[0] // FFN_ROW_SPLIT
    for r, h in enumerate(hs):
        rs = slice(r * rows, (r + 1) * rows)
        h_ref[rs, :] = h
        u = _rms(h, mix_g_ref[...]).astype(BF16)
        z = _dot(u, w_ref[...])
        o = 2 * MLSTM_QK
        mqk_ref[rs, :] = z[:, :o].astype(BF16)
        mv_ref[rs, :] = z[:, o:o + MLSTM_V].astype(BF16)
        o += MLSTM_V
        fq_ref[rs, :] = (z[:, o:o + FOX_W] * (FOX_DH ** -0.5 * LOG2E)).astype(BF16)
        fk_ref[rs, :] = z[:, o + FOX_W:o + 2 * FOX_W].astype(BF16)
        fvt_ref[0, :, rs] = z[:, o + 2 * FOX_W:o + 3 * FOX_W].T.astype(BF16)
        zs_ref[rs, :] = z[:, o + 3 * FOX_W:]


def _ffn_in_proj(x, ffn, mix_g, w, *, batch, seq):
    t, d = x.shape
    pre_g, w_gate, w_up, w_down, post_g = ffn
    d_ff = w_gate.shape[1]
    n = w.shape[1]
    tiles = seq // TOKEN_TILE
    assert seq % TOKEN_TILE == 0 and d_ff % FFN_CHUNK == 0
    row = lambda width: pl.BlockSpec((TOKEN_TILE, width), lambda i: (i, 0))
    widths = (2 * MLSTM_QK, MLSTM_V, FOX_W, FOX_W)
    return pl.pallas_call(
        _ffn_in_proj_kernel,
        grid=(t // TOKEN_TILE,),
        in_specs=[row(d), _resident((1, d)), _resident((d, d_ff)), _resident((d, d_ff)),
                  _resident((d_ff, d)), _resident((1, d)), _resident((1, d)), _resident((d, n))],
        out_specs=[row(d)] + [row(wd) for wd in widths]
        + [pl.BlockSpec((1, FOX_W, TOKEN_TILE), lambda i: (i // tiles, 0, i % tiles)), row(LANES)],
        out_shape=[jax.ShapeDtypeStruct((t, d), F32)] + [jax.ShapeDtypeStruct((t, wd), BF16) for wd in widths]
        + [jax.ShapeDtypeStruct((batch, FOX_W, seq), BF16), jax.ShapeDtypeStruct((t, LANES), F32)],
        compiler_params=_params("parallel"),
        name="ffn_in_proj",
    )(x, pre_g, w_gate, w_up, w_down, post_g, mix_g, w)


def _ffn_ple_kernel(x_ref, pre_g_ref, wg_ref, wu_ref, wd_ref, post_g_ref,
                    p_ref, ple_pre_g_ref, ple_wg_ref, ple_bg_ref, ple_wp_ref, ple_post_g_ref, o_ref):
    hs = _ffn_half_step(x_ref[...], pre_g_ref, wg_ref, wu_ref, wd_ref, post_g_ref)
    rows = x_ref.shape[0] // FFN_ROW_SPLIT
    for r, h in enumerate(hs):
        rs = slice(r * rows, (r + 1) * rows)
        u = _rms(h, ple_pre_g_ref[...]).astype(BF16)
        gate = jax.nn.sigmoid(_dot(u, ple_wg_ref[...]) + ple_bg_ref[...])
        emb = _dot(p_ref[rs, :].astype(BF16), ple_wp_ref[...])
        o_ref[rs, :] = h + _rms(gate * emb, ple_post_g_ref[...])


def _ffn_ple(x, ffn, p, ple):
    t, d = x.shape
    pre_g, w_gate, w_up, w_down, post_g = ffn
    ple_pre_g, ple_w_g, ple_b_g, ple_w_p, ple_post_g = ple
    d_ff = w_gate.shape[1]
    dp = p.shape[1]
    assert t % TOKEN_TILE == 0 and d_ff % FFN_CHUNK == 0
    row = lambda width: pl.BlockSpec((TOKEN_TILE, width), lambda i: (i, 0))
    return pl.pallas_call(
        _ffn_ple_kernel,
        grid=(t // TOKEN_TILE,),
        in_specs=[row(d), _resident((1, d)), _resident((d, d_ff)), _resident((d, d_ff)),
                  _resident((d_ff, d)), _resident((1, d)),
                  row(dp), _resident((1, d)), _resident((d, d)), _resident((1, d)), _resident((dp, d)),
                  _resident((1, d))],
        out_specs=row(d),
        out_shape=jax.ShapeDtypeStruct((t, d), F32),
        compiler_params=_params("parallel"),
        name="ffn_ple",
    )(x, pre_g, w_gate, w_up, w_down, post_g, p, ple_pre_g, ple_w_g, ple_b_g, ple_w_p, ple_post_g)


def _split3(x):
    hi = x.astype(BF16)
    r1 = x - hi.astype(F32)
    mid = r1.astype(BF16)
    lo = (r1 - mid.astype(F32)).astype(BF16)
    return hi, mid, lo


def _fox_aug_lane(hh, pair):
    return (FOX_DH if hh == 0 else 0) + FOX_AUG_STRIDE * pair


def _routing_constants():
    h_, sl = MLSTM_HEADS, MLSTM_SLOTS
    ra = np.zeros((N_SPLIT * LANES, LANES), np.float32)
    rb = np.zeros((N_SPLIT * LANES, LANES), np.float32)
    ones_a = np.zeros((1, LANES), np.float32)
    ones_b = np.zeros((1, LANES), np.float32)
    bconst = np.zeros((h_, LANES, 2 * LANES), np.float32)
    for h in range(h_):
        for j in range(N_SPLIT):
            ra[j * LANES + h_ + h, sl * h + j] = 1.0
            ra[j * LANES + h, sl * h + N_SPLIT + j] = 1.0
            rb[j * LANES + h, sl * h + 2 * N_SPLIT + j] = 1.0
            rb[j * LANES + h_ + h, sl * h + 3 * N_SPLIT + j] = -1.0
            bconst[h, sl * h + j, :LANES] = 1.0
            bconst[h, sl * h + j, LANES:] = -1.0
            bconst[h, sl * h + N_SPLIT + j, LANES:] = 1.0
        ones_a[0, sl * h + 2 * N_SPLIT:sl * h + 4 * N_SPLIT] = 1.0
        ones_b[0, sl * h:sl * h + N_SPLIT] = 1.0
    rk = np.zeros((N_SPLIT * LANES, LANES), np.float32)
    for h in range(FOX_HEADS):
        for j in range(N_SPLIT):
            rk[j * LANES + 2 * h_ + h, _fox_aug_lane(h % 2, h // 2) + j] = -1.0
    as_bf = lambda a: jnp.asarray(a, BF16)
    return as_bf(ra), jnp.asarray(ones_a), as_bf(rb), jnp.asarray(ones_b), as_bf(rk), as_bf(bconst)


def _gates_kernel(zs_ref, bias_ref, fk_ref, ra_ref, ones_a_ref, rb_ref, ones_b_ref, rk_ref,
                  ga_ref, gb_ref, kaug_ref, carry_ref):
    @pl.when(pl.program_id(1) == 0)
    def _():
        carry_ref[...] = jnp.zeros_like(carry_ref)

    x = zs_ref[0] + bias_ref[...]
    n = x.shape[0]
    log_f = jnp.minimum(x, 0.0) - jnp.log1p(jnp.exp(-jnp.abs(x)))
    L = MLSTM_CHUNK
    tri = jnp.where(lax.broadcasted_iota(jnp.int32, (L, L), 0) >= lax.broadcasted_iota(jnp.int32, (L, L), 1),
                    1.0, 0.0).astype(BF16)
    parts = _split3(log_f)
    cum_chunk = jnp.concatenate(
        [sum(_dot(tri, part[ci * L:(ci + 1) * L, :]) for part in parts) for ci in range(n // L)], axis=0)
    row = lax.broadcasted_iota(jnp.int32, x.shape, 0)
    before = carry_ref[0:1, :]
    offset = jnp.broadcast_to(before, x.shape)
    for ci in range(1, n // MLSTM_CHUNK):
        before = before + cum_chunk[ci * MLSTM_CHUNK - 1:ci * MLSTM_CHUNK, :]
        offset = jnp.where(row >= ci * MLSTM_CHUNK, before, offset)
    cum_all = cum_chunk + offset
    carry_ref[0:1, :] = cum_all[n - 1:n, :]
    col = lax.broadcasted_iota(jnp.int32, x.shape, 1)
    vals = jnp.where(col < MLSTM_HEADS, x, jnp.where(col < 2 * MLSTM_HEADS, cum_chunk, cum_all * LOG2E))

    split = jnp.concatenate(_split3(vals), axis=1)
    ga_ref[0] = (_dot(split, ra_ref[...]) + ones_a_ref[...]).astype(BF16)
    gb_ref[0] = (_dot(split, rb_ref[...]) + ones_b_ref[...]).T.astype(BF16)
    aug = _dot(split, rk_ref[...])
    lane = lax.broadcasted_iota(jnp.int32, (1, LANES), 1)
    lo_half = lane < FOX_DH
    for h in range(FOX_HEADS):
        a0 = _fox_aug_lane(h % 2, h // 2)
        mine = jnp.where(lane >= a0, jnp.where(lane < a0 + N_SPLIT, aug, 0.0), 0.0).astype(BF16)
        k_pair = fk_ref[0, :, (h // 2) * LANES:(h // 2 + 1) * LANES]
        own = lo_half if h % 2 == 0 else jnp.logical_not(lo_half)
        kaug_ref[0, :, h * LANES:(h + 1) * LANES] = jnp.where(own, k_pair, mine)


def _gates(zs, bias, fk, consts):
    b, s, _ = zs.shape
    ra, ones_a, rb, ones_b, rk, _ = consts
    assert s % GATE_TILE == 0 and GATE_TILE % MLSTM_CHUNK == 0
    tile = lambda width: pl.BlockSpec((1, GATE_TILE, width), lambda i, j: (i, j, 0))
    return pl.pallas_call(
        _gates_kernel,
        grid=(b, s // GATE_TILE),
        in_specs=[tile(LANES), pl.BlockSpec((1, LANES), lambda i, j: (0, 0)), tile(FOX_W),
                  _resident(ra.shape), _resident(ones_a.shape), _resident(rb.shape), _resident(ones_b.shape),
                  _resident(rk.shape)],
        out_specs=[tile(LANES), pl.BlockSpec((1, LANES, GATE_TILE), lambda i, j: (i, 0, j)),
                   tile(FOX_HEADS * LANES)],
        out_shape=[jax.ShapeDtypeStruct((b, s, LANES), BF16),
                   jax.ShapeDtypeStruct((b, LANES, s), BF16),
                   jax.ShapeDtypeStruct((b, s, FOX_HEADS * LANES), BF16)],
        scratch_shapes=[pltpu.VMEM((SUBLANES, LANES), F32)],
        compiler_params=_params("parallel", "arbitrary"),
        name="gates",
    )(zs, bias, fk, ra, ones_a, rb, ones_b, rk)


def _mlstm_kernel(qk_ref, v_ref, ga_ref, gb_ref, bconst_ref, cw_ref, cb_ref, ng_ref, o_ref,
                  tail_ref, c_ref, m_ref):
    @pl.when(pl.program_id(1) == 0)
    def _():
        tail_ref[...] = jnp.zeros_like(tail_ref)
        c_ref[...] = jnp.zeros_like(c_ref)
        m_ref[...] = jnp.zeros_like(m_ref)

    L = MLSTM_CHUNK
    rows = qk_ref.shape[1]

    cur = qk_ref[0].astype(F32)
    ext = jnp.concatenate([tail_ref[...], cur], axis=0)
    conv = cb_ref[...] + cw_ref[CONV_WIDTH - 1:CONV_WIDTH, :] * cur
    for d in range(1, CONV_WIDTH):
        shifted = pltpu.roll(ext, d, 0)[SUBLANES:, :]
        conv = conv + cw_ref[CONV_WIDTH - 1 - d:CONV_WIDTH - d, :] * shifted
    tail_ref[...] = cur[rows - SUBLANES:, :]
    qk = conv * jax.nn.sigmoid(conv)
    q_all = qk[:, :MLSTM_QK]
    k_all = qk[:, MLSTM_QK:] * (MLSTM_DQK ** -0.5)

    lane = lax.broadcasted_iota(jnp.int32, (1, LANES), 1)
    lo_half = lane < MLSTM_DQK
    srow = lax.broadcasted_iota(jnp.int32, (LANES, 1), 0) < MLSTM_DQK
    slot_owner = lax.broadcasted_iota(jnp.int32, (LANES, 1), 0) // MLSTM_SLOTS
    tril = (lax.broadcasted_iota(jnp.int32, (L, L), 0) >= lax.broadcasted_iota(jnp.int32, (L, L), 1))
    ones_block = jnp.ones((L, LANES), BF16)
    twice = lambda a: jnp.concatenate([a, a], axis=1)

    chunks = range(rows // L)
    heads = range(MLSTM_HEADS)
    units = [(ci, h) for ci in chunks for h in heads]
    rs = {ci: slice(ci * L, (ci + 1) * L) for ci in chunks}

    b_rep, log_d, m_intra, log_w, w_max, b_last = {}, {}, {}, {}, {}, {}
    for ci, h in units:
        key_side = jnp.where(slot_owner == h, gb_ref[0, :, rs[ci]], jnp.zeros((), BF16))
        g = _dot(ga_ref[0, rs[ci], :], jnp.concatenate([key_side, bconst_ref[h]], axis=1))
        log_d[ci, h] = jnp.where(tril, g[:, :L], -jnp.inf)
        b_rep[ci, h] = g[:, L:L + LANES]
        b_last[ci, h] = b_rep[ci, h][L - 1:L, :]
        m_intra[ci, h] = jnp.max(log_d[ci, h], axis=-1, keepdims=True)
        log_w[ci, h] = b_last[ci, h] + g[:, L + LANES:]
        w_max[ci, h] = jnp.max(log_w[ci, h], axis=0, keepdims=True)

    m_st, m_next, decay = {}, {}, {}
    for h in heads:
        m = m_ref[h:h + 1, :]
        for ci in chunks:
            m_st[ci, h] = m
            m = jnp.maximum(b_last[ci, h] + m, w_max[ci, h])
            m_next[ci, h] = m
            decay[ci, h] = twice(jnp.exp(b_last[ci, h] + m_st[ci, h] - m))
        m_ref[h:h + 1, :] = m

    q_h, k_pair, k_pair_b, v_aug = {}, {}, {}, {}
    for ci in chunks:
        for p in range(MLSTM_HEADS // 2):
            ps = slice(p * LANES, (p + 1) * LANES)
            k_pair[ci, p] = k_all[rs[ci], ps]
            k_pair_b[ci, p] = k_pair[ci, p].astype(BF16)
            q_pair = q_all[rs[ci], ps]
            q_h[ci, 2 * p] = jnp.where(lo_half, q_pair, 0.0).astype(BF16)
            q_h[ci, 2 * p + 1] = jnp.where(lo_half, 0.0, q_pair).astype(BF16)
        for h in heads:
            v_aug[ci, h] = jnp.concatenate([v_ref[0, rs[ci], h * MLSTM_DV:(h + 1) * MLSTM_DV], ones_block], axis=1)

    scores = {u: _dot_nt(q_h[u], k_pair_b[u[0], u[1] // 2]) for u in units}
    upd = {}
    for ci, h in units:
        w = jnp.exp(log_w[ci, h] - m_next[ci, h])
        upd[ci, h] = _dot_tn((k_pair[ci, h // 2] * w).astype(BF16), v_aug[ci, h])

    c_b = {}
    for p in range(MLSTM_HEADS // 2):
        c_pair = c_ref[p]
        for ci in chunks:
            c_b[ci, p] = c_pair.astype(BF16)
            c_pair = jnp.where(srow, decay[ci, 2 * p] * c_pair + upd[ci, 2 * p],
                               decay[ci, 2 * p + 1] * c_pair + upd[ci, 2 * p + 1])
        c_ref[p] = c_pair

    m_t, nd = {}, {}
    for ci, h in units:
        m_t[ci, h] = jnp.maximum(b_rep[ci, h] + m_st[ci, h], m_intra[ci, h])
        sc = (scores[ci, h] * jnp.exp(log_d[ci, h] - m_t[ci, h])).astype(BF16)
        inter = jnp.exp(b_rep[ci, h] + m_st[ci, h] - m_t[ci, h])
        nd[ci, h] = _dot(sc, v_aug[ci, h]) + twice(inter) * _dot(q_h[ci, h], c_b[ci, h // 2])

    hm = {}
    for u in units:
        den = jnp.maximum(jnp.abs(nd[u][:, MLSTM_DV:]), jnp.exp(-m_t[u]))
        hm[u] = nd[u][:, :MLSTM_DV] / den
    cen = {u: hm[u] - jnp.mean(hm[u], axis=-1, keepdims=True) for u in units}
    var = {u: jnp.mean(cen[u] * cen[u], axis=-1, keepdims=True) for u in units}
    for ci, h in units:
        cols = slice(h * MLSTM_DV, (h + 1) * MLSTM_DV)
        o_ref[0, rs[ci], cols] = (cen[ci, h] * lax.rsqrt(var[ci, h] + EPS) * ng_ref[:, cols]).astype(o_ref.dtype)


def _mlstm(mqk, mv, ga, gb, bconst, conv_w, conv_b, norm_g):
    b, s, _ = mqk.shape
    assert s % MLSTM_TILE == 0 and MLSTM_TILE % MLSTM_CHUNK == 0
    assert MLSTM_DV == LANES and MLSTM_HEADS * MLSTM_SLOTS <= LANES
    slab = lambda width: pl.BlockSpec((1, MLSTM_TILE, width), lambda i, j: (i, j, 0))
    return pl.pallas_call(
        _mlstm_kernel,
        grid=(b, s // MLSTM_TILE),
        in_specs=[slab(2 * MLSTM_QK), slab(MLSTM_V), slab(LANES),
                  pl.BlockSpec((1, LANES, MLSTM_TILE), lambda i, j: (i, 0, j)),
                  _resident(bconst.shape),
                  _resident((CONV_WIDTH, 2 * MLSTM_QK)), _resident((1, 2 * MLSTM_QK)),
                  _resident((1, MLSTM_V))],
        out_specs=slab(MLSTM_V),
        out_shape=jax.ShapeDtypeStruct((b, s, MLSTM_V), BF16),
        scratch_shapes=[pltpu.VMEM((SUBLANES, 2 * MLSTM_QK), F32),
                        pltpu.VMEM((MLSTM_HEADS // 2, LANES, 2 * LANES), F32),
                        pltpu.VMEM((SUBLANES, LANES), F32)],
        compiler_params=_params("parallel", "arbitrary"),
        name="mlstm",
    )(mqk, mv, ga, gb, bconst, conv_w, conv_b, norm_g)


def _fox_kernel(q_ref, k_ref, vt_ref, o_ref):
    tq = q_ref.shape[1]
    pair = pl.program_id(1)
    i = pl.program_id(2)
    q = q_ref[0]
    lane = lax.broadcasted_iota(jnp.int32, (1, LANES), 1)
    lo_half = lane < FOX_DH
    q_aug = []
    for hh in range(2):
        a0 = _fox_aug_lane(hh, pair)
        bias_lanes = jnp.where(lane >= a0, jnp.where(lane < a0 + N_SPLIT, 1.0, 0.0), 0.0).astype(BF16)
        q_aug.append(jnp.where(lo_half if hh == 0 else jnp.logical_not(lo_half), q, bias_lanes))

    def scores(off, w):
        return [_dot_nt(k_ref[0, off:off + w, hh * LANES:(hh + 1) * LANES], q_aug[hh]) for hh in range(2)]

    def absorb(block, off, w, s_pair, state):
        out = []
        for hh in range(2):
            m, l, acc = state[hh]
            s = s_pair[hh]
            if off + w > block * tq:
                key_pos = off + lax.broadcasted_iota(jnp.int32, (w, tq), 0)
                query_pos = block * tq + lax.broadcasted_iota(jnp.int32, (w, tq), 1)
                s = jnp.where(key_pos <= query_pos, s, -jnp.inf)
            m_new = jnp.maximum(m, jnp.max(s, axis=0, keepdims=True))
            alpha = jnp.exp2(m - m_new)
            p = jnp.exp2(s - m_new)
            l = alpha * l + jnp.sum(p, axis=0, keepdims=True)
            acc = alpha * acc + _dot(vt_ref[0, hh * FOX_DH:(hh + 1) * FOX_DH, off:off + w], p.astype(BF16))
            out.append((m_new, l, acc))
        return out

    def sweep(block):
        n_keys = (block + 1) * tq
        pieces = [(off, min(FOX_SUB, n_keys - off)) for off in range(0, n_keys, FOX_SUB)]
        state = [(jnp.full((1, tq), -jnp.inf, F32), jnp.zeros((1, tq), F32), jnp.zeros((FOX_DH, tq), F32))
                 for _ in range(2)]
        ahead = scores(*pieces[0])
        for k, (off, w) in enumerate(pieces):
            s_pair = ahead
            if k + 1 < len(pieces):
                ahead = scores(*pieces[k + 1])
            state = absorb(block, off, w, s_pair, state)
        (_, l0, a0), (_, l1, a1) = state
        return jnp.concatenate([a0 / l0, a1 / l1], axis=0).T.astype(o_ref.dtype)

    o_ref[0] = lax.switch(i, [lambda block=block: sweep(block) for block in range(k_ref.shape[1] // tq)])


def _fox(fq, kaug, fvt):
    b, s, _ = fq.shape
    assert s % FOX_TQ == 0 and FOX_SUB % FOX_TQ == 0 and 2 * FOX_DH == LANES
    pairs = FOX_HEADS // 2
    return pl.pallas_call(
        _fox_kernel,
        grid=(b, pairs, s // FOX_TQ),
        in_specs=[pl.BlockSpec((1, FOX_TQ, LANES), lambda bi, p, i: (bi, i, p)),
                  pl.BlockSpec((1, s, 2 * LANES), lambda bi, p, i: (bi, 0, p)),
                  pl.BlockSpec((1, LANES, s), lambda bi, p, i: (bi, p, 0))],
        out_specs=pl.BlockSpec((1, FOX_TQ, LANES), lambda bi, p, i: (bi, i, p)),
        out_shape=jax.ShapeDtypeStruct((b, s, FOX_W), BF16),
        compiler_params=_params("parallel", "parallel", "arbitrary"),
        name="fox",
    )(fq, kaug, fvt)


def _merge_kernel(h_ref, ya_ref, yb_ref, pre_g_ref, wu_ref, bias_ref, wa_ref, wb_ref, wo_ref, post_g_ref, o_ref):
    h = h_ref[...]
    d = h.shape[1]
    u = _rms(h, pre_g_ref[...]).astype(BF16)
    zu = _dot(u, wu_ref[...])
    y_a = (jax.nn.sigmoid(zu[:, :MLSTM_V]) * ya_ref[...].astype(F32)).astype(BF16)
    gates = jax.nn.sigmoid(zu[:, MLSTM_V:] + bias_ref[...])
    merged = gates[:, :d] * _dot(y_a, wa_ref[...]) + gates[:, d:] * _dot(yb_ref[...], wb_ref[...])
    o_ref[...] = h + _rms(_dot(merged.astype(BF16), wo_ref[...]), post_g_ref[...])


def _merge(h, ya, yb, pre_g, w_u, bias, w_a, w_b, w_o, post_g):
    t, d = h.shape
    row = lambda width: pl.BlockSpec((TOKEN_TILE, width), lambda i: (i, 0))
    return pl.pallas_call(
        _merge_kernel,
        grid=(t // TOKEN_TILE,),
        in_specs=[row(d), row(MLSTM_V), row(FOX_W), _resident((1, d)), _resident(w_u.shape),
                  _resident((1, 2 * d)), _resident(w_a.shape), _resident(w_b.shape), _resident((d, d)),
                  _resident((1, d))],
        out_specs=row(d),
        out_shape=jax.ShapeDtypeStruct((t, d), F32),
        compiler_params=_params("parallel"),
        name="merge",
    )(h, ya, yb, pre_g, w_u, bias, w_a, w_b, w_o, post_g)


def _layer(h, p, ffn1_pre_g, ffn1_w_gate, ffn1_w_up, ffn1_w_down, ffn1_post_g,
           mix_pre_g, w_in, conv_w, conv_b, mlstm_i_bias, mlstm_f_bias, mlstm_norm_g,
           fox_f_bias, branch_gate_bias, w_branch_a, w_branch_b, w_out, mix_post_g,
           ffn2_pre_g, ffn2_w_gate, ffn2_w_up, ffn2_w_down, ffn2_post_g,
           ple_pre_g, ple_w_gate, ple_b_gate, ple_w_proj, ple_post_g, *, batch, seq):
    vec = lambda a: a.reshape(1, -1)
    bf = lambda a: a.astype(BF16)

    o_mo = 2 * MLSTM_QK + MLSTM_V
    o_mi = o_mo + MLSTM_V
    o_fq = o_mi + 2 * MLSTM_HEADS
    o_ff = o_fq + 3 * FOX_W
    o_ga = o_ff + FOX_HEADS
    w_gate_cols = jnp.concatenate([w_in[:, o_mi:o_fq], w_in[:, o_ff:o_ga]], axis=1)
    w_gate_cols = jnp.pad(w_gate_cols, ((0, 0), (0, LANES - N_GATES)))
    w_streams = bf(jnp.concatenate([w_in[:, :o_mo], w_in[:, o_fq:o_ff], w_gate_cols], axis=1))
    w_late = bf(jnp.concatenate([w_in[:, o_mo:o_mi], w_in[:, o_ga:]], axis=1))
    gate_bias = jnp.pad(jnp.concatenate([mlstm_i_bias, mlstm_f_bias, fox_f_bias]), (0, LANES - N_GATES))
    consts = _routing_constants()

    ffn1 = (vec(ffn1_pre_g), bf(ffn1_w_gate), bf(ffn1_w_up), bf(ffn1_w_down), vec(ffn1_post_g))
    h1, mqk, mv, fq, fk, fvt, zs = _ffn_in_proj(h, ffn1, vec(mix_pre_g), w_streams, batch=batch, seq=seq)
    per_seq = lambda a: a.reshape(batch, seq, a.shape[-1])
    ga, gb, kaug = _gates(per_seq(zs), vec(gate_bias), per_seq(fk), consts)
    ya = _mlstm(per_seq(mqk), per_seq(mv), ga, gb, consts[-1], conv_w, vec(conv_b), vec(mlstm_norm_g))
    yb = _fox(per_seq(fq), kaug, fvt)

    h2 = _merge(h1, ya.reshape(batch * seq, MLSTM_V), yb.reshape(batch * seq, FOX_W), vec(mix_pre_g), w_late,
                vec(branch_gate_bias), bf(w_branch_a), bf(w_branch_b), bf(w_out), vec(mix_post_g))
    ffn2 = (vec(ffn2_pre_g), bf(ffn2_w_gate), bf(ffn2_w_up), bf(ffn2_w_down), vec(ffn2_post_g))
    ple = (vec(ple_pre_g), bf(ple_w_gate), vec(ple_b_gate), bf(ple_w_proj), vec(ple_post_g))
    return _ffn_ple(h2, ffn2, p, ple)


def kernel(x, p, ffn1_pre_g, ffn1_w_gate, ffn1_w_up, ffn1_w_down, ffn1_post_g, mix_pre_g, w_in, conv_w, conv_b, mlstm_i_bias, mlstm_f_bias, mlstm_norm_g, fox_f_bias, branch_gate_bias, w_branch_a, w_branch_b, w_out, mix_post_g, ffn2_pre_g, ffn2_w_gate, ffn2_w_up, ffn2_w_down, ffn2_post_g, ple_pre_g, ple_w_gate, ple_b_gate, ple_w_proj, ple_post_g):
    batch, seq, d = x.shape
    weights = (ffn1_pre_g, ffn1_w_gate, ffn1_w_up, ffn1_w_down, ffn1_post_g, mix_pre_g, w_in, conv_w, conv_b,
               mlstm_i_bias, mlstm_f_bias, mlstm_norm_g, fox_f_bias, branch_gate_bias, w_branch_a, w_branch_b,
               w_out, mix_post_g, ffn2_pre_g, ffn2_w_gate, ffn2_w_up, ffn2_w_down, ffn2_post_g, ple_pre_g,
               ple_w_gate, ple_b_gate, ple_w_proj, ple_post_g)
    h = x.reshape(batch * seq, d)
    for layer in range(p.shape[0]):
        h = _layer(h, p[layer].reshape(batch * seq, -1), *(w[layer] for w in weights), batch=batch, seq=seq)
    return h.reshape(batch, seq, d)
```

```python
import math

import jax
import jax.numpy as jnp
import numpy as np
from jax import lax
from jax.experimental import pallas as pl
from jax.experimental.pallas import tpu as pltpu

F32 = jnp.float32
BF16 = jnp.bfloat16

EPS = 1e-6
MLSTM_HEADS = 4
MLSTM_DQK = 64
MLSTM_DV = 128
MLSTM_QK = MLSTM_HEADS * MLSTM_DQK
MLSTM_V = MLSTM_HEADS * MLSTM_DV
CONV_WIDTH = 4
FOX_HEADS = 8
FOX_DH = 64
FOX_W = FOX_HEADS * FOX_DH
N_GATES = 2 * MLSTM_HEADS + FOX_HEADS
LOG2E = math.log2(math.e)

LANES = 128
SUBLANES = 8
VMEM_LIMIT_BYTES = 56 * 1024 * 1024

TOKEN_TILE = 1024
FFN_ROW_SPLIT = 2
FFN_CHUNK = 256
GATE_TILE = 4096
MLSTM_CHUNK = LANES
MLSTM_TILE = 1024
MLSTM_SLOTS = 16
N_SPLIT = 3
FOX_TQ = 512
FOX_SUB = 1024
FOX_AUG_STRIDE = 8


def _params(*semantics, flags=None):
    return pltpu.CompilerParams(dimension_semantics=semantics, vmem_limit_bytes=VMEM_LIMIT_BYTES, flags=flags)


def _resident(shape):
    nd = len(shape)
    return pl.BlockSpec(shape, lambda *_: (0,) * nd, pipeline_mode=pl.Buffered(1))


def _rms(x, g):
    return x * lax.rsqrt(jnp.mean(x * x, axis=-1, keepdims=True) + EPS) * g


def _dot(a, b):
    return jnp.dot(a, b, preferred_element_type=F32)


def _dot_nt(a, b):
    return lax.dot_general(a, b, (((1,), (1,)), ((), ())), preferred_element_type=F32)


def _dot_tn(a, b):
    return lax.dot_general(a, b, (((0,), (0,)), ((), ())), preferred_element_type=F32)


def _ffn_half_step(x, pre_g_ref, wg_ref, wu_ref, wd_ref, post_g_ref):
    d_ff = wg_ref.shape[1]
    rows = x.shape[0] // FFN_ROW_SPLIT
    xs = [x[r * rows:(r + 1) * rows, :] for r in range(FFN_ROW_SPLIT)]
    xn = [_rms(xr, pre_g_ref[...]).astype(BF16) for xr in xs]
    acc = [jnp.zeros(xr.shape, F32) for xr in xs]
    for c in range(d_ff // FFN_CHUNK):
        sl = slice(c * FFN_CHUNK, (c + 1) * FFN_CHUNK)
        for r in range(FFN_ROW_SPLIT):
            g = _dot(xn[r], wg_ref[:, sl])
            u = _dot(xn[r], wu_ref[:, sl])
            a = (g * jax.nn.sigmoid(g) * u).astype(BF16)
            acc[r] = acc[r] + _dot(a, wd_ref[sl, :])
    return [xr + 0.5 * _rms(ar, post_g_ref[...]) for xr, ar in zip(xs, acc)]


def _ffn_in_proj_kernel(x_ref, pre_g_ref, wg_ref, wu_ref, wd_ref, post_g_ref, mix_g_ref, w_ref,
                        h_ref, mqk_ref, mv_ref, fq_ref, fk_ref, fvt_ref, zs_ref):
    hs = _ffn_half_step(x_ref[...], pre_g_ref, wg_ref, wu_ref, wd_ref, post_g_ref)
    rows = x_ref.shape[0] // FFN_ROW_SPLIT
    for r, h in enumerate(hs):
        rs = slice(r * rows, (r + 1) * rows)
        h_ref[rs, :] = h
        u = _rms(h, mix_g_ref[...]).astype(BF16)
        z = _dot(u, w_ref[...])
        o = 2 * MLSTM_QK
        mqk_ref[rs, :] = z[:, :o].astype(BF16)
        mv_ref[rs, :] = z[:, o:o + MLSTM_V].astype(BF16)
        o += MLSTM_V
        fq_ref[rs, :] = (z[:, o:o + FOX_W] * (FOX_DH ** -0.5 * LOG2E)).astype(BF16)
        fk_ref[rs, :] = z[:, o + FOX_W:o + 2 * FOX_W].astype(BF16)
        fvt_ref[0, :, rs] = z[:, o + 2 * FOX_W:o + 3 * FOX_W].T.astype(BF16)
        zs_ref[rs, :] = z[:, o + 3 * FOX_W:]


def _ffn_in_proj(x, ffn, mix_g, w, *, batch, seq):
    t, d = x.shape
    pre_g, w_gate, w_up, w_down, post_g = ffn
    d_ff = w_gate.shape[1]
    n = w.shape[1]
    tiles = seq // TOKEN_TILE
    assert seq % TOKEN_TILE == 0 and d_ff % FFN_CHUNK == 0
    row = lambda width: pl.BlockSpec((TOKEN_TILE, width), lambda i: (i, 0))
    widths = (2 * MLSTM_QK, MLSTM_V, FOX_W, FOX_W)
    return pl.pallas_call(
        _ffn_in_proj_kernel,
        grid=(t // TOKEN_TILE,),
        in_specs=[row(d), _resident((1, d)), _resident((d, d_ff)), _resident((d, d_ff)),
                  _resident((d_ff, d)), _resident((1, d)), _resident((1, d)), _resident((d, n))],
        out_specs=[row(d)] + [row(wd) for wd in widths]
        + [pl.BlockSpec((1, FOX_W, TOKEN_TILE), lambda i: (i // tiles, 0, i % tiles)), row(LANES)],
        out_shape=[jax.ShapeDtypeStruct((t, d), F32)] + [jax.ShapeDtypeStruct((t, wd), BF16) for wd in widths]
        + [jax.ShapeDtypeStruct((batch, FOX_W, seq), BF16), jax.ShapeDtypeStruct((t, LANES), F32)],
        compiler_params=_params("parallel"),
        name="ffn_in_proj",
    )(x, pre_g, w_gate, w_up, w_down, post_g, mix_g, w)


def _ffn_ple_kernel(x_ref, pre_g_ref, wg_ref, wu_ref, wd_ref, post_g_ref,
                    p_ref, ple_pre_g_ref, ple_wg_ref, ple_bg_ref, ple_wp_ref, ple_post_g_ref, o_ref):
    hs = _ffn_half_step(x_ref[...], pre_g_ref, wg_ref, wu_ref, wd_ref, post_g_ref)
    rows = x_ref.shape[0] // FFN_ROW_SPLIT
    for r, h in enumerate(hs):
        rs = slice(r * rows, (r + 1) * rows)
        u = _rms(h, ple_pre_g_ref[...]).astype(BF16)
        gate = jax.nn.sigmoid(_dot(u, ple_wg_ref[...]) + ple_bg_ref[...])
        emb = _dot(p_ref[rs, :].astype(BF16), ple_wp_ref[...])
        o_ref[rs, :] = h + _rms(gate * emb, ple_post_g_ref[...])


def _ffn_ple(x, ffn, p, ple):
    t, d = x.shape
    pre_g, w_gate, w_up, w_down, post_g = ffn
    ple_pre_g, ple_w_g, ple_b_g, ple_w_p, ple_post_g = ple
    d_ff = w_gate.shape[1]
    dp = p.shape[1]
    assert t % TOKEN_TILE == 0 and d_ff % FFN_CHUNK == 0
    row = lambda width: pl.BlockSpec((TOKEN_TILE, width), lambda i: (i, 0))
    return pl.pallas_call(
        _ffn_ple_kernel,
        grid=(t // TOKEN_TILE,),
        in_specs=[row(d), _resident((1, d)), _resident((d, d_ff)), _resident((d, d_ff)),
                  _resident((d_ff, d)), _resident((1, d)),
                  row(dp), _resident((1, d)), _resident((d, d)), _resident((1, d)), _resident((dp, d)),
                  _resident((1, d))],
        out_specs=row(d),
        out_shape=jax.ShapeDtypeStruct((t, d), F32),
        compiler_params=_params("parallel"),
        name="ffn_ple",
    )(x, pre_g, w_gate, w_up, w_down, post_g, p, ple_pre_g, ple_w_g, ple_b_g, ple_w_p, ple_post_g)


def _split3(x):
    hi = x.astype(BF16)
    r1 = x - hi.astype(F32)
    mid = r1.astype(BF16)
    lo = (r1 - mid.astype(F32)).astype(BF16)
    return hi, mid, lo


def _fox_aug_lane(hh, pair):
    return (FOX_DH if hh == 0 else 0) + FOX_AUG_STRIDE * pair


def _routing_constants():
    h_, sl = MLSTM_HEADS, MLSTM_SLOTS
    ra = np.zeros((N_SPLIT * LANES, LANES), np.float32)
    rb = np.zeros((N_SPLIT * LANES, LANES), np.float32)
    ones_a = np.zeros((1, LANES), np.float32)
    ones_b = np.zeros((1, LANES), np.float32)
    bconst = np.zeros((h_, LANES, 2 * LANES), np.float32)
    for h in range(h_):
        for j in range(N_SPLIT):
            ra[j * LANES + h_ + h, sl * h + j] = 1.0
            ra[j * LANES + h, sl * h + N_SPLIT + j] = 1.0
            rb[j * LANES + h, sl * h + 2 * N_SPLIT + j] = 1.0
            rb[j * LANES + h_ + h, sl * h + 3 * N_SPLIT + j] = -1.0
            bconst[h, sl * h + j, :LANES] = 1.0
            bconst[h, sl * h + j, LANES:] = -1.0
            bconst[h, sl * h + N_SPLIT + j, LANES:] = 1.0
        ones_a[0, sl * h + 2 * N_SPLIT:sl * h + 4 * N_SPLIT] = 1.0
        ones_b[0, sl * h:sl * h + N_SPLIT] = 1.0
    rk = np.zeros((N_SPLIT * LANES, LANES), np.float32)
    for h in range(FOX_HEADS):
        for j in range(N_SPLIT):
            rk[j * LANES + 2 * h_ + h, _fox_aug_lane(h % 2, h // 2) + j] = -1.0
    as_bf = lambda a: jnp.asarray(a, BF16)
    return as_bf(ra), jnp.asarray(ones_a), as_bf(rb), jnp.asarray(ones_b), as_bf(rk), as_bf(bconst)


def _gates_kernel(zs_ref, bias_ref, fk_ref, ra_ref, ones_a_ref, rb_ref, ones_b_ref, rk_ref,
                  ga_ref, gb_ref, kaug_ref, carry_ref):
    @pl.when(pl.program_id(1) == 0)
    def _():
        carry_ref[...] = jnp.zeros_like(carry_ref)

    x = zs_ref[0] + bias_ref[...]
    n = x.shape[0]
    log_f = jnp.minimum(x, 0.0) - jnp.log1p(jnp.exp(-jnp.abs(x)))
    L = MLSTM_CHUNK
    tri = jnp.where(lax.broadcasted_iota(jnp.int32, (L, L), 0) >= lax.broadcasted_iota(jnp.int32, (L, L), 1),
                    1.0, 0.0).astype(BF16)
    parts = _split3(log_f)
    cum_chunk = jnp.concatenate(
        [sum(_dot(tri, part[ci * L:(ci + 1) * L, :]) for part in parts) for ci in range(n // L)], axis=0)
    row = lax.broadcasted_iota(jnp.int32, x.shape, 0)
    before = carry_ref[0:1, :]
    offset = jnp.broadcast_to(before, x.shape)
    for ci in range(1, n // MLSTM_CHUNK):
        before = before + cum_chunk[ci * MLSTM_CHUNK - 1:ci * MLSTM_CHUNK, :]
        offset = jnp.where(row >= ci * MLSTM_CHUNK, before, offset)
    cum_all = cum_chunk + offset
    carry_ref[0:1, :] = cum_all[n - 1:n, :]
    col = lax.broadcasted_iota(jnp.int32, x.shape, 1)
    vals = jnp.where(col < MLSTM_HEADS, x, jnp.where(col < 2 * MLSTM_HEADS, cum_chunk, cum_all * LOG2E))

    split = jnp.concatenate(_split3(vals), axis=1)
    ga_ref[0] = (_dot(split, ra_ref[...]) + ones_a_ref[...]).astype(BF16)
    gb_ref[0] = (_dot(split, rb_ref[...]) + ones_b_ref[...]).T.astype(BF16)
    aug = _dot(split, rk_ref[...])
    lane = lax.broadcasted_iota(jnp.int32, (1, LANES), 1)
    lo_half = lane < FOX_DH
    for h in range(FOX_HEADS):
        a0 = _fox_aug_lane(h % 2, h // 2)
        mine = jnp.where(lane >= a0, jnp.where(lane < a0 + N_SPLIT, aug, 0.0), 0.0).astype(BF16)
        k_pair = fk_ref[0, :, (h // 2) * LANES:(h // 2 + 1) * LANES]
        own = lo_half if h % 2 == 0 else jnp.logical_not(lo_half)
        kaug_ref[0, :, h * LANES:(h + 1) * LANES] = jnp.where(own, k_pair, mine)


def _gates(zs, bias, fk, consts):
    b, s, _ = zs.shape
    ra, ones_a, rb, ones_b, rk, _ = consts
    assert s % GATE_TILE == 0 and GATE_TILE % MLSTM_CHUNK == 0
    tile = lambda width: pl.BlockSpec((1, GATE_TILE, width), lambda i, j: (i, j, 0))
    return pl.pallas_call(
        _gates_kernel,
        grid=(b, s // GATE_TILE),
        in_specs=[tile(LANES), pl.BlockSpec((1, LANES), lambda i, j: (0, 0)), tile(FOX_W),
                  _resident(ra.shape), _resident(ones_a.shape), _resident(rb.shape), _resident(ones_b.shape),
                  _resident(rk.shape)],
        out_specs=[tile(LANES), pl.BlockSpec((1, LANES, GATE_TILE), lambda i, j: (i, 0, j)),
                   tile(FOX_HEADS * LANES)],
        out_shape=[jax.ShapeDtypeStruct((b, s, LANES), BF16),
                   jax.ShapeDtypeStruct((b, LANES, s), BF16),
                   jax.ShapeDtypeStruct((b, s, FOX_HEADS * LANES), BF16)],
        scratch_shapes=[pltpu.VMEM((SUBLANES, LANES), F32)],
        compiler_params=_params("parallel", "arbitrary"),
        name="gates",
    )(zs, bias, fk, ra, ones_a, rb, ones_b, rk)


def _mlstm_kernel(qk_ref, v_ref, ga_ref, gb_ref, bconst_ref, cw_ref, cb_ref, ng_ref, o_ref,
                  tail_ref, c_ref, m_ref):
    @pl.when(pl.program_id(1) == 0)
    def _():
        tail_ref[...] = jnp.zeros_like(tail_ref)
        c_ref[...] = jnp.zeros_like(c_ref)
        m_ref[...] = jnp.zeros_like(m_ref)

    L = MLSTM_CHUNK
    rows = qk_ref.shape[1]

    cur = qk_ref[0].astype(F32)
    ext = jnp.concatenate([tail_ref[...], cur], axis=0)
    conv = cb_ref[...] + cw_ref[CONV_WIDTH - 1:CONV_WIDTH, :] * cur
    for d in range(1, CONV_WIDTH):
        shifted = pltpu.roll(ext, d, 0)[SUBLANES:, :]
        conv = conv + cw_ref[CONV_WIDTH - 1 - d:CONV_WIDTH - d, :] * shifted
    tail_ref[...] = cur[rows - SUBLANES:, :]
    qk = conv * jax.nn.sigmoid(conv)
    q_all = qk[:, :MLSTM_QK]
    k_all = qk[:, MLSTM_QK:] * (MLSTM_DQK ** -0.5)

    lane = lax.broadcasted_iota(jnp.int32, (1, LANES), 1)
    lo_half = lane < MLSTM_DQK
    srow = lax.broadcasted_iota(jnp.int32, (LANES, 1), 0) < MLSTM_DQK
    slot_owner = lax.broadcasted_iota(jnp.int32, (LANES, 1), 0) // MLSTM_SLOTS
    tril = (lax.broadcasted_iota(jnp.int32, (L, L), 0) >= lax.broadcasted_iota(jnp.int32, (L, L), 1))
    ones_block = jnp.ones((L, LANES), BF16)
    twice = lambda a: jnp.concatenate([a, a], axis=1)

    chunks = range(rows // L)
    heads = range(MLSTM_HEADS)
    units = [(ci, h) for ci in chunks for h in heads]
    rs = {ci: slice(ci * L, (ci + 1) * L) for ci in chunks}

    b_rep, log_d, m_intra, log_w, w_max, b_last = {}, {}, {}, {}, {}, {}
    for ci, h in units:
        key_side = jnp.where(slot_owner == h, gb_ref[0, :, rs[ci]], jnp.zeros((), BF16))
        g = _dot(ga_ref[0, rs[ci], :], jnp.concatenate([key_side, bconst_ref[h]], axis=1))
        log_d[ci, h] = jnp.where(tril, g[:, :L], -jnp.inf)
        b_rep[ci, h] = g[:, L:L + LANES]
        b_last[ci, h] = b_rep[ci, h][L - 1:L, :]
        m_intra[ci, h] = jnp.max(log_d[ci, h], axis=-1, keepdims=True)
        log_w[ci, h] = b_last[ci, h] + g[:, L + LANES:]
        w_max[ci, h] = jnp.max(log_w[ci, h], axis=0, keepdims=True)

    m_st, m_next, decay = {}, {}, {}
    for h in heads:
        m = m_ref[h:h + 1, :]
        for ci in chunks:
            m_st[ci, h] = m
            m = jnp.maximum(b_last[ci, h] + m, w_max[ci, h])
            m_next[ci, h] = m
            decay[ci, h] = twice(jnp.exp(b_last[ci, h] + m_st[ci, h] - m))
        m_ref[h:h + 1, :] = m

    q_h, k_pair, k_pair_b, v_aug = {}, {}, {}, {}
    for ci in chunks:
        for p in range(MLSTM_HEADS // 2):
            ps = slice(p * LANES, (p + 1) * LANES)
            k_pair[ci, p] = k_all[rs[ci], ps]
            k_pair_b[ci, p] = k_pair[ci, p].astype(BF16)
            q_pair = q_all[rs[ci], ps]
            q_h[ci, 2 * p] = jnp.where(lo_half, q_pair, 0.0).astype(BF16)
            q_h[ci, 2 * p + 1] = jnp.where(lo_half, 0.0, q_pair).astype(BF16)
        for h in heads:
            v_aug[ci, h] = jnp.concatenate([v_ref[0, rs[ci], h * MLSTM_DV:(h + 1) * MLSTM_DV], ones_block], axis=1)

    scores = {u: _dot_nt(q_h[u], k_pair_b[u[0], u[1] // 2]) for u in units}
    upd = {}
    for ci, h in units:
        w = jnp.exp(log_w[ci, h] - m_next[ci, h])
        upd[ci, h] = _dot_tn((k_pair[ci, h // 2] * w).astype(BF16), v_aug[ci, h])

    c_b = {}
    for p in range(MLSTM_HEADS // 2):
        c_pair = c_ref[p]
        for ci in chunks:
            c_b[ci, p] = c_pair.astype(BF16)
            c_pair = jnp.where(srow, decay[ci, 2 * p] * c_pair + upd[ci, 2 * p],
                               decay[ci, 2 * p + 1] * c_pair + upd[ci, 2 * p + 1])
        c_ref[p] = c_pair

    m_t, nd = {}, {}
    for ci, h in units:
        m_t[ci, h] = jnp.maximum(b_rep[ci, h] + m_st[ci, h], m_intra[ci, h])
        sc = (scores[ci, h] * jnp.exp(log_d[ci, h] - m_t[ci, h])).astype(BF16)
        inter = jnp.exp(b_rep[ci, h] + m_st[ci, h] - m_t[ci, h])
        nd[ci, h] = _dot(sc, v_aug[ci, h]) + twice(inter) * _dot(q_h[ci, h], c_b[ci, h // 2])

    hm = {}
    for u in units:
        den = jnp.maximum(jnp.abs(nd[u][:, MLSTM_DV:]), jnp.exp(-m_t[u]))
        hm[u] = nd[u][:, :MLSTM_DV] / den
    cen = {u: hm[u] - jnp.mean(hm[u], axis=-1, keepdims=True) for u in units}
    var = {u: jnp.mean(cen[u] * cen[u], axis=-1, keepdims=True) for u in units}
    for ci, h in units:
        cols = slice(h * MLSTM_DV, (h + 1) * MLSTM_DV)
        o_ref[0, rs[ci], cols] = (cen[ci, h] * lax.rsqrt(var[ci, h] + EPS) * ng_ref[:, cols]).astype(o_ref.dtype)


def _mlstm(mqk, mv, ga, gb, bconst, conv_w, conv_b, norm_g):
    b, s, _ = mqk.shape
    assert s % MLSTM_TILE == 0 and MLSTM_TILE % MLSTM_CHUNK == 0
    assert MLSTM_DV == LANES and MLSTM_HEADS * MLSTM_SLOTS <= LANES
    slab = lambda width: pl.BlockSpec((1, MLSTM_TILE, width), lambda i, j: (i, j, 0))
    return pl.pallas_call(
        _mlstm_kernel,
        grid=(b, s // MLSTM_TILE),
        in_specs=[slab(2 * MLSTM_QK), slab(MLSTM_V), slab(LANES),
                  pl.BlockSpec((1, LANES, MLSTM_TILE), lambda i, j: (i, 0, j)),
                  _resident(bconst.shape),
                  _resident((CONV_WIDTH, 2 * MLSTM_QK)), _resident((1, 2 * MLSTM_QK)),
                  _resident((1, MLSTM_V))],
        out_specs=slab(MLSTM_V),
        out_shape=jax.ShapeDtypeStruct((b, s, MLSTM_V), BF16),
        scratch_shapes=[pltpu.VMEM((SUBLANES, 2 * MLSTM_QK), F32),
                        pltpu.VMEM((MLSTM_HEADS // 2, LANES, 2 * LANES), F32),
                        pltpu.VMEM((SUBLANES, LANES), F32)],
        compiler_params=_params("parallel", "arbitrary"),
        name="mlstm",
    )(mqk, mv, ga, gb, bconst, conv_w, conv_b, norm_g)


def _fox_kernel(q_ref, k_ref, vt_ref, o_ref):
    tq = q_ref.shape[1]
    pair = pl.program_id(1)
    i = pl.program_id(2)
    q = q_ref[0]
    lane = lax.broadcasted_iota(jnp.int32, (1, LANES), 1)
    lo_half = lane < FOX_DH
    q_aug = []
    for hh in range(2):
        a0 = _fox_aug_lane(hh, pair)
        bias_lanes = jnp.where(lane >= a0, jnp.where(lane < a0 + N_SPLIT, 1.0, 0.0), 0.0).astype(BF16)
        q_aug.append(jnp.where(lo_half if hh == 0 else jnp.logical_not(lo_half), q, bias_lanes))

    def scores(off, w):
        return [_dot_nt(k_ref[0, off:off + w, hh * LANES:(hh + 1) * LANES], q_aug[hh]) for hh in range(2)]

    def absorb(block, off, w, s_pair, state):
        out = []
        for hh in range(2):
            m, l, acc = state[hh]
            s = s_pair[hh]
            if off + w > block * tq:
                key_pos = off + lax.broadcasted_iota(jnp.int32, (w, tq), 0)
                query_pos = block * tq + lax.broadcasted_iota(jnp.int32, (w, tq), 1)
                s = jnp.where(key_pos <= query_pos, s, -jnp.inf)
            m_new = jnp.maximum(m, jnp.max(s, axis=0, keepdims=True))
            alpha = jnp.exp2(m - m_new)
            p = jnp.exp2(s - m_new)
            l = alpha * l + jnp.sum(p, axis=0, keepdims=True)
            acc = alpha * acc + _dot(vt_ref[0, hh * FOX_DH:(hh + 1) * FOX_DH, off:off + w], p.astype(BF16))
            out.append((m_new, l, acc))
        return out

    def sweep(block):
        n_keys = (block + 1) * tq
        pieces = [(off, min(FOX_SUB, n_keys - off)) for off in range(0, n_keys, FOX_SUB)]
        state = [(jnp.full((1, tq), -jnp.inf, F32), jnp.zeros((1, tq), F32), jnp.zeros((FOX_DH, tq), F32))
                 for _ in range(2)]
        ahead = scores(*pieces[0])
        for k, (off, w) in enumerate(pieces):
            s_pair = ahead
            if k + 1 < len(pieces):
                ahead = scores(*pieces[k + 1])
            state = absorb(block, off, w, s_pair, state)
        (_, l0, a0), (_, l1, a1) = state
        return jnp.concatenate([a0 / l0, a1 / l1], axis=0).T.astype(o_ref.dtype)

    o_ref[0] = lax.switch(i, [lambda block=block: sweep(block) for block in range(k_ref.shape[1] // tq)])


def _fox(fq, kaug, fvt):
    b, s, _ = fq.shape
    assert s % FOX_TQ == 0 and FOX_SUB % FOX_TQ == 0 and 2 * FOX_DH == LANES
    pairs = FOX_HEADS // 2
    return pl.pallas_call(
        _fox_kernel,
        grid=(b, pairs, s // FOX_TQ),
        in_specs=[pl.BlockSpec((1, FOX_TQ, LANES), lambda bi, p, i: (bi, i, p)),
                  pl.BlockSpec((1, s, 2 * LANES), lambda bi, p, i: (bi, 0, p)),
                  pl.BlockSpec((1, LANES, s), lambda bi, p, i: (bi, p, 0))],
        out_specs=pl.BlockSpec((1, FOX_TQ, LANES), lambda bi, p, i: (bi, i, p)),
        out_shape=jax.ShapeDtypeStruct((b, s, FOX_W), BF16),
        compiler_params=_params("parallel", "parallel", "arbitrary"),
        name="fox",
    )(fq, kaug, fvt)


def _merge_kernel(h_ref, ya_ref, yb_ref, pre_g_ref, wu_ref, bias_ref, wa_ref, wb_ref, wo_ref, post_g_ref, o_ref):
    h = h_ref[...]
    d = h.shape[1]
    u = _rms(h, pre_g_ref[...]).astype(BF16)
    zu = _dot(u, wu_ref[...])
    y_a = (jax.nn.sigmoid(zu[:, :MLSTM_V]) * ya_ref[...].astype(F32)).astype(BF16)
    gates = jax.nn.sigmoid(zu[:, MLSTM_V:] + bias_ref[...])
    merged = gates[:, :d] * _dot(y_a, wa_ref[...]) + gates[:, d:] * _dot(yb_ref[...], wb_ref[...])
    o_ref[...] = h + _rms(_dot(merged.astype(BF16), wo_ref[...]), post_g_ref[...])


def _merge(h, ya, yb, pre_g, w_u, bias, w_a, w_b, w_o, post_g):
    t, d = h.shape
    row = lambda width: pl.BlockSpec((TOKEN_TILE, width), lambda i: (i, 0))
    return pl.pallas_call(
        _merge_kernel,
        grid=(t // TOKEN_TILE,),
        in_specs=[row(d), row(MLSTM_V), row(FOX_W), _resident((1, d)), _resident(w_u.shape),
                  _resident((1, 2 * d)), _resident(w_a.shape), _resident(w_b.shape), _resident((d, d)),
                  _resident((1, d))],
        out_specs=row(d),
        out_shape=jax.ShapeDtypeStruct((t, d), F32),
        compiler_params=_params("parallel"),
        name="merge",
    )(h, ya, yb, pre_g, w_u, bias, w_a, w_b, w_o, post_g)


def _layer(h, p, ffn1_pre_g, ffn1_w_gate, ffn1_w_up, ffn1_w_down, ffn1_post_g,
           mix_pre_g, w_in, conv_w, conv_b, mlstm_i_bias, mlstm_f_bias, mlstm_norm_g,
           fox_f_bias, branch_gate_bias, w_branch_a, w_branch_b, w_out, mix_post_g,
           ffn2_pre_g, ffn2_w_gate, ffn2_w_up, ffn2_w_down, ffn2_post_g,
           ple_pre_g, ple_w_gate, ple_b_gate, ple_w_proj, ple_post_g, *, batch, seq):
    vec = lambda a: a.reshape(1, -1)
    bf = lambda a: a.astype(BF16)

    o_mo = 2 * MLSTM_QK + MLSTM_V
    o_mi = o_mo + MLSTM_V
    o_fq = o_mi + 2 * MLSTM_HEADS
    o_ff = o_fq + 3 * FOX_W
    o_ga = o_ff + FOX_HEADS
    w_gate_cols = jnp.concatenate([w_in[:, o_mi:o_fq], w_in[:, o_ff:o_ga]], axis=1)
    w_gate_cols = jnp.pad(w_gate_cols, ((0, 0), (0, LANES - N_GATES)))
    w_streams = bf(jnp.concatenate([w_in[:, :o_mo], w_in[:, o_fq:o_ff], w_gate_cols], axis=1))
    w_late = bf(jnp.concatenate([w_in[:, o_mo:o_mi], w_in[:, o_ga:]], axis=1))
    gate_bias = jnp.pad(jnp.concatenate([mlstm_i_bias, mlstm_f_bias, fox_f_bias]), (0, LANES - N_GATES))
    consts = _routing_constants()

    ffn1 = (vec(ffn1_pre_g), bf(ffn1_w_gate), bf(ffn1_w_up), bf(ffn1_w_down), vec(ffn1_post_g))
    h1, mqk, mv, fq, fk, fvt, zs = _ffn_in_proj(h, ffn1, vec(mix_pre_g), w_streams, batch=batch, seq=seq)
    per_seq = lambda a: a.reshape(batch, seq, a.shape[-1])
    ga, gb, kaug = _gates(per_seq(zs), vec(gate_bias), per_seq(fk), consts)
    ya = _mlstm(per_seq(mqk), per_seq(mv), ga, gb, consts[-1], conv_w, vec(conv_b), vec(mlstm_norm_g))
    yb = _fox(per_seq(fq), kaug, fvt)

    h2 = _merge(h1, ya.reshape(batch * seq, MLSTM_V), yb.reshape(batch * seq, FOX_W), vec(mix_pre_g), w_late,
                vec(branch_gate_bias), bf(w_branch_a), bf(w_branch_b), bf(w_out), vec(mix_post_g))
    ffn2 = (vec(ffn2_pre_g), bf(ffn2_w_gate), bf(ffn2_w_up), bf(ffn2_w_down), vec(ffn2_post_g))
    ple = (vec(ple_pre_g), bf(ple_w_gate), vec(ple_b_gate), bf(ple_w_proj), vec(ple_post_g))
    return _ffn_ple(h2, ffn2, p, ple)


def kernel(x, p, ffn1_pre_g, ffn1_w_gate, ffn1_w_up, ffn1_w_down, ffn1_post_g, mix_pre_g, w_in, conv_w, conv_b, mlstm_i_bias, mlstm_f_bias, mlstm_norm_g, fox_f_bias, branch_gate_bias, w_branch_a, w_branch_b, w_out, mix_post_g, ffn2_pre_g, ffn2_w_gate, ffn2_w_up, ffn2_w_down, ffn2_post_g, ple_pre_g, ple_w_gate, ple_b_gate, ple_w_proj, ple_post_g):
    batch, seq, d = x.shape
    weights = (ffn1_pre_g, ffn1_w_gate, ffn1_w_up, ffn1_w_down, ffn1_post_g, mix_pre_g, w_in, conv_w, conv_b,
               mlstm_i_bias, mlstm_f_bias, mlstm_norm_g, fox_f_bias, branch_gate_bias, w_branch_a, w_branch_b,
               w_out, mix_post_g, ffn2_pre_g, ffn2_w_gate, ffn2_w_up, ffn2_w_down, ffn2_post_g, ple_pre_g,
               ple_w_gate, ple_b_gate, ple_w_proj, ple_post_g)
    h = x.reshape(batch * seq, d)
    for layer in range(p.shape[0]):
        h = _layer(h, p[layer].reshape(batch * seq, -1), *(w[layer] for w in weights), batch=batch, seq=seq)
    return h.reshape(batch, seq, d)
```

```python
import math

import jax
import jax.numpy as jnp
import numpy as np
from jax import lax
from jax.experimental import pallas as pl
from jax.experimental.pallas import tpu as pltpu

F32 = jnp.float32
BF16 = jnp.bfloat16

EPS = 1e-6
MLSTM_HEADS = 4
MLSTM_DQK = 64
MLSTM_DV = 128
MLSTM_QK = MLSTM_HEADS * MLSTM_DQK
MLSTM_V = MLSTM_HEADS * MLSTM_DV
CONV_WIDTH = 4
FOX_HEADS = 8
FOX_DH = 64
FOX_W = FOX_HEADS * FOX_DH
N_GATES = 2 * MLSTM_HEADS + FOX_HEADS
LOG2E = math.log2(math.e)

LANES = 128
SUBLANES = 8
VMEM_LIMIT_BYTES = 56 * 1024 * 1024

TOKEN_TILE = 1024
FFN_ROW_SPLIT = 2
FFN_CHUNK = 256
GATE_TILE = 4096
MLSTM_CHUNK = LANES
MLSTM_TILE = 1024
MLSTM_SLOTS = 16
N_SPLIT = 3
FOX_TQ = 512
FOX_SUB = 1024
FOX_AUG_STRIDE = 8


def _params(*semantics):
    return pltpu.CompilerParams(dimension_semantics=semantics, vmem_limit_bytes=VMEM_LIMIT_BYTES)


def _resident(shape):
    nd = len(shape)
    return pl.BlockSpec(shape, lambda *_: (0,) * nd, pipeline_mode=pl.Buffered(1))


def _rms(x, g):
    return x * lax.rsqrt(jnp.mean(x * x, axis=-1, keepdims=True) + EPS) * g


def _dot(a, b):
    return jnp.dot(a, b, preferred_element_type=F32)


def _dot_nt(a, b):
    return lax.dot_general(a, b, (((1,), (1,)), ((), ())), preferred_element_type=F32)


def _dot_tn(a, b):
    return lax.dot_general(a, b, (((0,), (0,)), ((), ())), preferred_element_type=F32)


def _ffn_half_step(x, pre_g_ref, wg_ref, wu_ref, wd_ref, post_g_ref):
    d_ff = wg_ref.shape[1]
    rows = x.shape[0] // FFN_ROW_SPLIT
    xs = [x[r * rows:(r + 1) * rows, :] for r in range(FFN_ROW_SPLIT)]
    xn = [_rms(xr, pre_g_ref[...]).astype(BF16) for xr in xs]
    acc = [jnp.zeros(xr.shape, F32) for xr in xs]
    for c in range(d_ff // FFN_CHUNK):
        sl = slice(c * FFN_CHUNK, (c + 1) * FFN_CHUNK)
        for r in range(FFN_ROW_SPLIT):
            g = _dot(xn[r], wg_ref[:, sl])
            u = _dot(xn[r], wu_ref[:, sl])
            a = (g * jax.nn.sigmoid(g) * u).astype(BF16)
            acc[r] = acc[r] + _dot(a, wd_ref[sl, :])
    return [xr + 0.5 * _rms(ar, post_g_ref[...]) for xr, ar in zip(xs, acc)]


def _ffn_in_proj_kernel(x_ref, pre_g_ref, wg_ref, wu_ref, wd_ref, post_g_ref, mix_g_ref, w_ref,
                        h_ref, mqk_ref, mv_ref, fq_ref, fk_ref, fvt_ref, zs_ref):
    hs = _ffn_half_step(x_ref[...], pre_g_ref, wg_ref, wu_ref, wd_ref, post_g_ref)
    rows = x_ref.shape[0] // FFN_ROW_SPLIT
    for r, h in enumerate(hs):
        rs = slice(r * rows, (r + 1) * rows)
        h_ref[rs, :] = h
        u = _rms(h, mix_g_ref[...]).astype(BF16)
        z = _dot(u, w_ref[...])
        o = 2 * MLSTM_QK
        mqk_ref[rs, :] = z[:, :o].astype(BF16)
        mv_ref[rs, :] = z[:, o:o + MLSTM_V].astype(BF16)
        o += MLSTM_V
        fq_ref[rs, :] = (z[:, o:o + FOX_W] * (FOX_DH ** -0.5 * LOG2E)).astype(BF16)
        fk_ref[rs, :] = z[:, o + FOX_W:o + 2 * FOX_W].astype(BF16)
        fvt_ref[0, :, rs] = z[:, o + 2 * FOX_W:o + 3 * FOX_W].T.astype(BF16)
        zs_ref[rs, :] = z[:, o + 3 * FOX_W:]


def _ffn_in_proj(x, ffn, mix_g, w, *, batch, seq):
    t, d = x.shape
    pre_g, w_gate, w_up, w_down, post_g = ffn
    d_ff = w_gate.shape[1]
    n = w.shape[1]
    tiles = seq // TOKEN_TILE
    assert seq % TOKEN_TILE == 0 and d_ff % FFN_CHUNK == 0
    row = lambda width: pl.BlockSpec((TOKEN_TILE, width), lambda i: (i, 0))
    widths = (2 * MLSTM_QK, MLSTM_V, FOX_W, FOX_W)
    return pl.pallas_call(
        _ffn_in_proj_kernel,
        grid=(t // TOKEN_TILE,),
        in_specs=[row(d), _resident((1, d)), _resident((d, d_ff)), _resident((d, d_ff)),
                  _resident((d_ff, d)), _resident((1, d)), _resident((1, d)), _resident((d, n))],
        out_specs=[row(d)] + [row(wd) for wd in widths]
        + [pl.BlockSpec((1, FOX_W, TOKEN_TILE), lambda i: (i // tiles, 0, i % tiles)), row(LANES)],
        out_shape=[jax.ShapeDtypeStruct((t, d), F32)] + [jax.ShapeDtypeStruct((t, wd), BF16) for wd in widths]
        + [jax.ShapeDtypeStruct((batch, FOX_W, seq), BF16), jax.ShapeDtypeStruct((t, LANES), F32)],
        compiler_params=_params("parallel"),
        name="ffn_in_proj",
    )(x, pre_g, w_gate, w_up, w_down, post_g, mix_g, w)


def _ffn_ple_kernel(x_ref, pre_g_ref, wg_ref, wu_ref, wd_ref, post_g_ref,
                    p_ref, ple_pre_g_ref, ple_wg_ref, ple_bg_ref, ple_wp_ref, ple_post_g_ref, o_ref):
    hs = _ffn_half_step(x_ref[...], pre_g_ref, wg_ref, wu_ref, wd_ref, post_g_ref)
    rows = x_ref.shape[0] // FFN_ROW_SPLIT
    for r, h in enumerate(hs):
        rs = slice(r * rows, (r + 1) * rows)
        u = _rms(h, ple_pre_g_ref[...]).astype(BF16)
        gate = jax.nn.sigmoid(_dot(u, ple_wg_ref[...]) + ple_bg_ref[...])
        emb = _dot(p_ref[rs, :].astype(BF16), ple_wp_ref[...])
        o_ref[rs, :] = h + _rms(gate * emb, ple_post_g_ref[...])


def _ffn_ple(x, ffn, p, ple):
    t, d = x.shape
    pre_g, w_gate, w_up, w_down, post_g = ffn
    ple_pre_g, ple_w_g, ple_b_g, ple_w_p, ple_post_g = ple
    d_ff = w_gate.shape[1]
    dp = p.shape[1]
    assert t % TOKEN_TILE == 0 and d_ff % FFN_CHUNK == 0
    row = lambda width: pl.BlockSpec((TOKEN_TILE, width), lambda i: (i, 0))
    return pl.pallas_call(
        _ffn_ple_kernel,
        grid=(t // TOKEN_TILE,),
        in_specs=[row(d), _resident((1, d)), _resident((d, d_ff)), _resident((d, d_ff)),
                  _resident((d_ff, d)), _resident((1, d)),
                  row(dp), _resident((1, d)), _resident((d, d)), _resident((1, d)), _resident((dp, d)),
                  _resident((1, d))],
        out_specs=row(d),
        out_shape=jax.ShapeDtypeStruct((t, d), F32),
        compiler_params=_params("parallel"),
        name="ffn_ple",
    )(x, pre_g, w_gate, w_up, w_down, post_g, p, ple_pre_g, ple_w_g, ple_b_g, ple_w_p, ple_post_g)


def _split3(x):
    hi = x.astype(BF16)
    r1 = x - hi.astype(F32)
    mid = r1.astype(BF16)
    lo = (r1 - mid.astype(F32)).astype(BF16)
    return hi, mid, lo


def _fox_aug_lane(hh, pair):
    return (FOX_DH if hh == 0 else 0) + FOX_AUG_STRIDE * pair


def _routing_constants():
    h_, sl = MLSTM_HEADS, MLSTM_SLOTS
    ra = np.zeros((N_SPLIT * LANES, LANES), np.float32)
    rb = np.zeros((N_SPLIT * LANES, LANES), np.float32)
    ones_a = np.zeros((1, LANES), np.float32)
    ones_b = np.zeros((1, LANES), np.float32)
    bconst = np.zeros((h_, LANES, 2 * LANES), np.float32)
    for h in range(h_):
        for j in range(N_SPLIT):
            ra[j * LANES + h_ + h, sl * h + j] = 1.0
            ra[j * LANES + h, sl * h + N_SPLIT + j] = 1.0
            rb[j * LANES + h, sl * h + 2 * N_SPLIT + j] = 1.0
            rb[j * LANES + h_ + h, sl * h + 3 * N_SPLIT + j] = -1.0
            bconst[h, sl * h + j, :LANES] = 1.0
            bconst[h, sl * h + j, LANES:] = -1.0
            bconst[h, sl * h + N_SPLIT + j, LANES:] = 1.0
        ones_a[0, sl * h + 2 * N_SPLIT:sl * h + 4 * N_SPLIT] = 1.0
        ones_b[0, sl * h:sl * h + N_SPLIT] = 1.0
    rk = np.zeros((N_SPLIT * LANES, LANES), np.float32)
    for h in range(FOX_HEADS):
        for j in range(N_SPLIT):
            rk[j * LANES + 2 * h_ + h, _fox_aug_lane(h % 2, h // 2) + j] = -1.0
    as_bf = lambda a: jnp.asarray(a, BF16)
    return as_bf(ra), jnp.asarray(ones_a), as_bf(rb), jnp.asarray(ones_b), as_bf(rk), as_bf(bconst)


def _gates_kernel(zs_ref, bias_ref, fk_ref, ra_ref, ones_a_ref, rb_ref, ones_b_ref, rk_ref,
                  ga_ref, gb_ref, kaug_ref, carry_ref):
    @pl.when(pl.program_id(1) == 0)
    def _():
        carry_ref[...] = jnp.zeros_like(carry_ref)

    x = zs_ref[0] + bias_ref[...]
    n = x.shape[0]
    log_f = jnp.minimum(x, 0.0) - jnp.log1p(jnp.exp(-jnp.abs(x)))
    L = MLSTM_CHUNK
    tri = jnp.where(lax.broadcasted_iota(jnp.int32, (L, L), 0) >= lax.broadcasted_iota(jnp.int32, (L, L), 1),
                    1.0, 0.0).astype(BF16)
    parts = _split3(log_f)
    cum_chunk = jnp.concatenate(
        [sum(_dot(tri, part[ci * L:(ci + 1) * L, :]) for part in parts) for ci in range(n // L)], axis=0)
    row = lax.broadcasted_iota(jnp.int32, x.shape, 0)
    before = carry_ref[0:1, :]
    offset = jnp.broadcast_to(before, x.shape)
    for ci in range(1, n // MLSTM_CHUNK):
        before = before + cum_chunk[ci * MLSTM_CHUNK - 1:ci * MLSTM_CHUNK, :]
        offset = jnp.where(row >= ci * MLSTM_CHUNK, before, offset)
    cum_all = cum_chunk + offset
    carry_ref[0:1, :] = cum_all[n - 1:n, :]
    col = lax.broadcasted_iota(jnp.int32, x.shape, 1)
    vals = jnp.where(col < MLSTM_HEADS, x, jnp.where(col < 2 * MLSTM_HEADS, cum_chunk, cum_all * LOG2E))

    split = jnp.concatenate(_split3(vals), axis=1)
    ga_ref[0] = (_dot(split, ra_ref[...]) + ones_a_ref[...]).astype(BF16)
    gb_ref[0] = (_dot(split, rb_ref[...]) + ones_b_ref[...]).T.astype(BF16)
    aug = _dot(split, rk_ref[...])
    lane = lax.broadcasted_iota(jnp.int32, (1, LANES), 1)
    lo_half = lane < FOX_DH
    for h in range(FOX_HEADS):
        a0 = _fox_aug_lane(h % 2, h // 2)
        mine = jnp.where(lane >= a0, jnp.where(lane < a0 + N_SPLIT, aug, 0.0), 0.0).astype(BF16)
        k_pair = fk_ref[0, :, (h // 2) * LANES:(h // 2 + 1) * LANES]
        own = lo_half if h % 2 == 0 else jnp.logical_not(lo_half)
        kaug_ref[0, :, h * LANES:(h + 1) * LANES] = jnp.where(own, k_pair, mine)


def _gates(zs, bias, fk, consts):
    b, s, _ = zs.shape
    ra, ones_a, rb, ones_b, rk, _ = consts
    assert s % GATE_TILE == 0 and GATE_TILE % MLSTM_CHUNK == 0
    tile = lambda width: pl.BlockSpec((1, GATE_TILE, width), lambda i, j: (i, j, 0))
    return pl.pallas_call(
        _gates_kernel,
        grid=(b, s // GATE_TILE),
        in_specs=[tile(LANES), pl.BlockSpec((1, LANES), lambda i, j: (0, 0)), tile(FOX_W),
                  _resident(ra.shape), _resident(ones_a.shape), _resident(rb.shape), _resident(ones_b.shape),
                  _resident(rk.shape)],
        out_specs=[tile(LANES), pl.BlockSpec((1, LANES, GATE_TILE), lambda i, j: (i, 0, j)),
                   tile(FOX_HEADS * LANES)],
        out_shape=[jax.ShapeDtypeStruct((b, s, LANES), BF16),
                   jax.ShapeDtypeStruct((b, LANES, s), BF16),
                   jax.ShapeDtypeStruct((b, s, FOX_HEADS * LANES), BF16)],
        scratch_shapes=[pltpu.VMEM((SUBLANES, LANES), F32)],
        compiler_params=_params("parallel", "arbitrary"),
        name="gates",
    )(zs, bias, fk, ra, ones_a, rb, ones_b, rk)


def _mlstm_kernel(qk_ref, v_ref, ga_ref, gb_ref, bconst_ref, cw_ref, cb_ref, ng_ref, o_ref,
                  tail_ref, c_ref, m_ref):
    @pl.when(pl.program_id(1) == 0)
    def _():
        tail_ref[...] = jnp.zeros_like(tail_ref)
        c_ref[...] = jnp.zeros_like(c_ref)
        m_ref[...] = jnp.zeros_like(m_ref)

    L = MLSTM_CHUNK
    rows = qk_ref.shape[1]

    cur = qk_ref[0].astype(F32)
    ext = jnp.concatenate([tail_ref[...], cur], axis=0)
    conv = cb_ref[...] + cw_ref[CONV_WIDTH - 1:CONV_WIDTH, :] * cur
    for d in range(1, CONV_WIDTH):
        shifted = pltpu.roll(ext, d, 0)[SUBLANES:, :]
        conv = conv + cw_ref[CONV_WIDTH - 1 - d:CONV_WIDTH - d, :] * shifted
    tail_ref[...] = cur[rows - SUBLANES:, :]
    qk = conv * jax.nn.sigmoid(conv)
    q_all = qk[:, :MLSTM_QK]
    k_all = qk[:, MLSTM_QK:] * (MLSTM_DQK ** -0.5)

    lane = lax.broadcasted_iota(jnp.int32, (1, LANES), 1)
    lo_half = lane < MLSTM_DQK
    srow = lax.broadcasted_iota(jnp.int32, (LANES, 1), 0) < MLSTM_DQK
    slot_owner = lax.broadcasted_iota(jnp.int32, (LANES, 1), 0) // MLSTM_SLOTS
    tril = (lax.broadcasted_iota(jnp.int32, (L, L), 0) >= lax.broadcasted_iota(jnp.int32, (L, L), 1))
    ones_block = jnp.ones((L, LANES), BF16)
    twice = lambda a: jnp.concatenate([a, a], axis=1)

    chunks = range(rows // L)
    heads = range(MLSTM_HEADS)
    units = [(ci, h) for ci in chunks for h in heads]
    rs = {ci: slice(ci * L, (ci + 1) * L) for ci in chunks}

    b_rep, log_d, m_intra, log_w, w_max, b_last = {}, {}, {}, {}, {}, {}
    for ci, h in units:
        key_side = jnp.where(slot_owner == h, gb_ref[0, :, rs[ci]], jnp.zeros((), BF16))
        g = _dot(ga_ref[0, rs[ci], :], jnp.concatenate([key_side, bconst_ref[h]], axis=1))
        log_d[ci, h] = jnp.where(tril, g[:, :L], -jnp.inf)
        b_rep[ci, h] = g[:, L:L + LANES]
        b_last[ci, h] = b_rep[ci, h][L - 1:L, :]
        m_intra[ci, h] = jnp.max(log_d[ci, h], axis=-1, keepdims=True)
        log_w[ci, h] = b_last[ci, h] + g[:, L + LANES:]
        w_max[ci, h] = jnp.max(log_w[ci, h], axis=0, keepdims=True)

    m_st, m_next, decay = {}, {}, {}
    for h in heads:
        m = m_ref[h:h + 1, :]
        for ci in chunks:
            m_st[ci, h] = m
            m = jnp.maximum(b_last[ci, h] + m, w_max[ci, h])
            m_next[ci, h] = m
            decay[ci, h] = twice(jnp.exp(b_last[ci, h] + m_st[ci, h] - m))
        m_ref[h:h + 1, :] = m

    q_h, k_pair, k_pair_b, v_aug = {}, {}, {}, {}
    for ci in chunks:
        for p in range(MLSTM_HEADS // 2):
            ps = slice(p * LANES, (p + 1) * LANES)
            k_pair[ci, p] = k_all[rs[ci], ps]
            k_pair_b[ci, p] = k_pair[ci, p].astype(BF16)
            q_pair = q_all[rs[ci], ps]
            q_h[ci, 2 * p] = jnp.where(lo_half, q_pair, 0.0).astype(BF16)
            q_h[ci, 2 * p + 1] = jnp.where(lo_half, 0.0, q_pair).astype(BF16)
        for h in heads:
            v_aug[ci, h] = jnp.concatenate([v_ref[0, rs[ci], h * MLSTM_DV:(h + 1) * MLSTM_DV], ones_block], axis=1)

    scores = {u: _dot_nt(q_h[u], k_pair_b[u[0], u[1] // 2]) for u in units}
    upd = {}
    for ci, h in units:
        w = jnp.exp(log_w[ci, h] - m_next[ci, h])
        upd[ci, h] = _dot_tn((k_pair[ci, h // 2] * w).astype(BF16), v_aug[ci, h])

    c_b = {}
    for p in range(MLSTM_HEADS // 2):
        c_pair = c_ref[p]
        for ci in chunks:
            c_b[ci, p] = c_pair.astype(BF16)
            c_pair = jnp.where(srow, decay[ci, 2 * p] * c_pair + upd[ci, 2 * p],
                               decay[ci, 2 * p + 1] * c_pair + upd[ci, 2 * p + 1])
        c_ref[p] = c_pair

    m_t, nd = {}, {}
    for ci, h in units:
        m_t[ci, h] = jnp.maximum(b_rep[ci, h] + m_st[ci, h], m_intra[ci, h])
        sc = (scores[ci, h] * jnp.exp(log_d[ci, h] - m_t[ci, h])).astype(BF16)
        inter = jnp.exp(b_rep[ci, h] + m_st[ci, h] - m_t[ci, h])
        nd[ci, h] = _dot(sc, v_aug[ci, h]) + twice(inter) * _dot(q_h[ci, h], c_b[ci, h // 2])

    hm = {}
    for u in units:
        den = jnp.maximum(jnp.abs(nd[u][:, MLSTM_DV:]), jnp.exp(-m_t[u]))
        hm[u] = nd[u][:, :MLSTM_DV] / den
    cen = {u: hm[u] - jnp.mean(hm[u], axis=-1, keepdims=True) for u in units}
    var = {u: jnp.mean(cen[u] * cen[u], axis=-1, keepdims=True) for u in units}
    for ci, h in units:
        cols = slice(h * MLSTM_DV, (h + 1) * MLSTM_DV)
        o_ref[0, rs[ci], cols] = (cen[ci, h] * lax.rsqrt(var[ci, h] + EPS) * ng_ref[:, cols]).astype(o_ref.dtype)


def _mlstm(mqk, mv, ga, gb, bconst, conv_w, conv_b, norm_g):
    b, s, _ = mqk.shape
    assert s % MLSTM_TILE == 0 and MLSTM_TILE % MLSTM_CHUNK == 0
    assert MLSTM_DV == LANES and MLSTM_HEADS * MLSTM_SLOTS <= LANES
    slab = lambda width: pl.BlockSpec((1, MLSTM_TILE, width), lambda i, j: (i, j, 0))
    return pl.pallas_call(
        _mlstm_kernel,
        grid=(b, s // MLSTM_TILE),
        in_specs=[slab(2 * MLSTM_QK), slab(MLSTM_V), slab(LANES),
                  pl.BlockSpec((1, LANES, MLSTM_TILE), lambda i, j: (i, 0, j)),
                  _resident(bconst.shape),
                  _resident((CONV_WIDTH, 2 * MLSTM_QK)), _resident((1, 2 * MLSTM_QK)),
                  _resident((1, MLSTM_V))],
        out_specs=slab(MLSTM_V),
        out_shape=jax.ShapeDtypeStruct((b, s, MLSTM_V), BF16),
        scratch_shapes=[pltpu.VMEM((SUBLANES, 2 * MLSTM_QK), F32),
                        pltpu.VMEM((MLSTM_HEADS // 2, LANES, 2 * LANES), F32),
                        pltpu.VMEM((SUBLANES, LANES), F32)],
        compiler_params=_params("parallel", "arbitrary"),
        name="mlstm",
    )(mqk, mv, ga, gb, bconst, conv_w, conv_b, norm_g)


def _fox_kernel(q_ref, k_ref, vt_ref, o_ref):
    tq = q_ref.shape[1]
    pair = pl.program_id(1)
    i = pl.program_id(2)
    q = q_ref[0]
    lane = lax.broadcasted_iota(jnp.int32, (1, LANES), 1)
    lo_half = lane < FOX_DH
    q_aug = []
    for hh in range(2):
        a0 = _fox_aug_lane(hh, pair)
        bias_lanes = jnp.where(lane >= a0, jnp.where(lane < a0 + N_SPLIT, 1.0, 0.0), 0.0).astype(BF16)
        q_aug.append(jnp.where(lo_half if hh == 0 else jnp.logical_not(lo_half), q, bias_lanes))

    def scores(hh, off, w):
        return _dot_nt(k_ref[0, off:off + w, hh * LANES:(hh + 1) * LANES], q_aug[hh])

    def absorb(block, hh, off, w, s, state):
        m, l, acc = state
        if off + w > block * tq:
            key_pos = off + lax.broadcasted_iota(jnp.int32, (w, tq), 0)
            query_pos = block * tq + lax.broadcasted_iota(jnp.int32, (w, tq), 1)
            s = jnp.where(key_pos <= query_pos, s, -jnp.inf)
        m_new = jnp.maximum(m, jnp.max(s, axis=0, keepdims=True))
        alpha = jnp.exp2(m - m_new)
        p = jnp.exp2(s - m_new)
        l = alpha * l + jnp.sum(p, axis=0, keepdims=True)
        acc = alpha * acc + _dot(vt_ref[0, hh * FOX_DH:(hh + 1) * FOX_DH, off:off + w], p.astype(BF16))
        return m_new, l, acc

    def sweep(block):
        n_keys = (block + 1) * tq
        pieces = [(off, min(FOX_SUB, n_keys - off)) for off in range(0, n_keys, FOX_SUB)]
        state = [(jnp.full((1, tq), -jnp.inf, F32), jnp.zeros((1, tq), F32), jnp.zeros((FOX_DH, tq), F32))
                 for _ in range(2)]
        ahead = [scores(hh, *pieces[0]) for hh in range(2)]
        for k, (off, w) in enumerate(pieces):
            for hh in range(2):
                s = ahead[hh]
                if k + 1 < len(pieces):
                    ahead[hh] = scores(hh, *pieces[k + 1])
                state[hh] = absorb(block, hh, off, w, s, state[hh])
        (_, l0, a0), (_, l1, a1) = state
        return jnp.concatenate([a0 / l0, a1 / l1], axis=0).T.astype(o_ref.dtype)

    o_ref[0] = lax.switch(i, [lambda block=block: sweep(block) for block in range(k_ref.shape[1] // tq)])


def _fox(fq, kaug, fvt):
    b, s, _ = fq.shape
    assert s % FOX_TQ == 0 and FOX_SUB % FOX_TQ == 0 and 2 * FOX_DH == LANES
    pairs = FOX_HEADS // 2
    return pl.pallas_call(
        _fox_kernel,
        grid=(b, pairs, s // FOX_TQ),
        in_specs=[pl.BlockSpec((1, FOX_TQ, LANES), lambda bi, p, i: (bi, i, p)),
                  pl.BlockSpec((1, s, 2 * LANES), lambda bi, p, i: (bi, 0, p)),
                  pl.BlockSpec((1, LANES, s), lambda bi, p, i: (bi, p, 0))],
        out_specs=pl.BlockSpec((1, FOX_TQ, LANES), lambda bi, p, i: (bi, i, p)),
        out_shape=jax.ShapeDtypeStruct((b, s, FOX_W), BF16),
        compiler_params=_params("parallel", "parallel", "arbitrary"),
        name="fox",
    )(fq, kaug, fvt)


def _merge_kernel(h_ref, ya_ref, yb_ref, pre_g_ref, wu_ref, bias_ref, wa_ref, wb_ref, wo_ref, post_g_ref, o_ref):
    h = h_ref[...]
    d = h.shape[1]
    u = _rms(h, pre_g_ref[...]).astype(BF16)
    zu = _dot(u, wu_ref[...])
    y_a = (jax.nn.sigmoid(zu[:, :MLSTM_V]) * ya_ref[...].astype(F32)).astype(BF16)
    gates = jax.nn.sigmoid(zu[:, MLSTM_V:] + bias_ref[...])
    merged = gates[:, :d] * _dot(y_a, wa_ref[...]) + gates[:, d:] * _dot(yb_ref[...], wb_ref[...])
    o_ref[...] = h + _rms(_dot(merged.astype(BF16), wo_ref[...]), post_g_ref[...])


def _merge(h, ya, yb, pre_g, w_u, bias, w_a, w_b, w_o, post_g):
    t, d = h.shape
    row = lambda width: pl.BlockSpec((TOKEN_TILE, width), lambda i: (i, 0))
    return pl.pallas_call(
        _merge_kernel,
        grid=(t // TOKEN_TILE,),
        in_specs=[row(d), row(MLSTM_V), row(FOX_W), _resident((1, d)), _resident(w_u.shape),
                  _resident((1, 2 * d)), _resident(w_a.shape), _resident(w_b.shape), _resident((d, d)),
                  _resident((1, d))],
        out_specs=row(d),
        out_shape=jax.ShapeDtypeStruct((t, d), F32),
        compiler_params=_params("parallel"),
        name="merge",
    )(h, ya, yb, pre_g, w_u, bias, w_a, w_b, w_o, post_g)


def _layer(h, p, ffn1_pre_g, ffn1_w_gate, ffn1_w_up, ffn1_w_down, ffn1_post_g,
           mix_pre_g, w_in, conv_w, conv_b, mlstm_i_bias, mlstm_f_bias, mlstm_norm_g,
           fox_f_bias, branch_gate_bias, w_branch_a, w_branch_b, w_out, mix_post_g,
           ffn2_pre_g, ffn2_w_gate, ffn2_w_up, ffn2_w_down, ffn2_post_g,
           ple_pre_g, ple_w_gate, ple_b_gate, ple_w_proj, ple_post_g, *, batch, seq):
    vec = lambda a: a.reshape(1, -1)
    bf = lambda a: a.astype(BF16)

    o_mo = 2 * MLSTM_QK + MLSTM_V
    o_mi = o_mo + MLSTM_V
    o_fq = o_mi + 2 * MLSTM_HEADS
    o_ff = o_fq + 3 * FOX_W
    o_ga = o_ff + FOX_HEADS
    w_gate_cols = jnp.concatenate([w_in[:, o_mi:o_fq], w_in[:, o_ff:o_ga]], axis=1)
    w_gate_cols = jnp.pad(w_gate_cols, ((0, 0), (0, LANES - N_GATES)))
    w_streams = bf(jnp.concatenate([w_in[:, :o_mo], w_in[:, o_fq:o_ff], w_gate_cols], axis=1))
    w_late = bf(jnp.concatenate([w_in[:, o_mo:o_mi], w_in[:, o_ga:]], axis=1))
    gate_bias = jnp.pad(jnp.concatenate([mlstm_i_bias, mlstm_f_bias, fox_f_bias]), (0, LANES - N_GATES))
    consts = _routing_constants()

    ffn1 = (vec(ffn1_pre_g), bf(ffn1_w_gate), bf(ffn1_w_up), bf(ffn1_w_down), vec(ffn1_post_g))
    h1, mqk, mv, fq, fk, fvt, zs = _ffn_in_proj(h, ffn1, vec(mix_pre_g), w_streams, batch=batch, seq=seq)
    per_seq = lambda a: a.reshape(batch, seq, a.shape[-1])
    ga, gb, kaug = _gates(per_seq(zs), vec(gate_bias), per_seq(fk), consts)
    ya = _mlstm(per_seq(mqk), per_seq(mv), ga, gb, consts[-1], conv_w, vec(conv_b), vec(mlstm_norm_g))
    yb = _fox(per_seq(fq), kaug, fvt)

    h2 = _merge(h1, ya.reshape(batch * seq, MLSTM_V), yb.reshape(batch * seq, FOX_W), vec(mix_pre_g), w_late,
                vec(branch_gate_bias), bf(w_branch_a), bf(w_branch_b), bf(w_out), vec(mix_post_g))
    ffn2 = (vec(ffn2_pre_g), bf(ffn2_w_gate), bf(ffn2_w_up), bf(ffn2_w_down), vec(ffn2_post_g))
    ple = (vec(ple_pre_g), bf(ple_w_gate), vec(ple_b_gate), bf(ple_w_proj), vec(ple_post_g))
    return _ffn_ple(h2, ffn2, p, ple)


def kernel(x, p, ffn1_pre_g, ffn1_w_gate, ffn1_w_up, ffn1_w_down, ffn1_post_g, mix_pre_g, w_in, conv_w, conv_b, mlstm_i_bias, mlstm_f_bias, mlstm_norm_g, fox_f_bias, branch_gate_bias, w_branch_a, w_branch_b, w_out, mix_post_g, ffn2_pre_g, ffn2_w_gate, ffn2_w_up, ffn2_w_down, ffn2_post_g, ple_pre_g, ple_w_gate, ple_b_gate, ple_w_proj, ple_post_g):
    batch, seq, d = x.shape
    weights = (ffn1_pre_g, ffn1_w_gate, ffn1_w_up, ffn1_w_down, ffn1_post_g, mix_pre_g, w_in, conv_w, conv_b,
               mlstm_i_bias, mlstm_f_bias, mlstm_norm_g, fox_f_bias, branch_gate_bias, w_branch_a, w_branch_b,
               w_out, mix_post_g, ffn2_pre_g, ffn2_w_gate, ffn2_w_up, ffn2_w_down, ffn2_post_g, ple_pre_g,
               ple_w_gate, ple_b_gate, ple_w_proj, ple_post_g)
    h = x.reshape(batch * seq, d)
    for layer in range(p.shape[0]):
        h = _layer(h, p[layer].reshape(batch * seq, -1), *(w[layer] for w in weights), batch=batch, seq=seq)
    return h.reshape(batch, seq, d)
```

```python
import math

import jax
import jax.numpy as jnp
import numpy as np
from jax import lax
from jax.experimental import pallas as pl
from jax.experimental.pallas import tpu as pltpu

F32 = jnp.float32
BF16 = jnp.bfloat16

EPS = 1e-6
MLSTM_HEADS = 4
MLSTM_DQK = 64
MLSTM_DV = 128
MLSTM_QK = MLSTM_HEADS * MLSTM_DQK
MLSTM_V = MLSTM_HEADS * MLSTM_DV
CONV_WIDTH = 4
FOX_HEADS = 8
FOX_DH = 64
FOX_W = FOX_HEADS * FOX_DH
N_GATES = 2 * MLSTM_HEADS + FOX_HEADS
LOG2E = math.log2(math.e)

LANES = 128
SUBLANES = 8
VMEM_LIMIT_BYTES = 56 * 1024 * 1024

TOKEN_TILE = 1024
FFN_ROW_SPLIT = 2
FFN_CHUNK = 256
GATE_TILE = 4096
MLSTM_CHUNK = LANES
MLSTM_TILE = 1024
MLSTM_SLOTS = 16
N_SPLIT = 3
FOX_TQ = 512
FOX_STEP_HEADS = 4
FOX_SUB = 1024
FOX_AUG_STRIDE = 8


def _params(*semantics):
    return pltpu.CompilerParams(dimension_semantics=semantics, vmem_limit_bytes=VMEM_LIMIT_BYTES)


def _resident(shape):
    nd = len(shape)
    return pl.BlockSpec(shape, lambda *_: (0,) * nd, pipeline_mode=pl.Buffered(1))


def _rms(x, g):
    return x * lax.rsqrt(jnp.mean(x * x, axis=-1, keepdims=True) + EPS) * g


def _dot(a, b):
    return jnp.dot(a, b, preferred_element_type=F32)


def _dot_nt(a, b):
    return lax.dot_general(a, b, (((1,), (1,)), ((), ())), preferred_element_type=F32)


def _dot_tn(a, b):
    return lax.dot_general(a, b, (((0,), (0,)), ((), ())), preferred_element_type=F32)


def _ffn_half_step(x, pre_g_ref, wg_ref, wu_ref, wd_ref, post_g_ref):
    d_ff = wg_ref.shape[1]
    rows = x.shape[0] // FFN_ROW_SPLIT
    xs = [x[r * rows:(r + 1) * rows, :] for r in range(FFN_ROW_SPLIT)]
    xn = [_rms(xr, pre_g_ref[...]).astype(BF16) for xr in xs]
    acc = [jnp.zeros(xr.shape, F32) for xr in xs]
    for c in range(d_ff // FFN_CHUNK):
        sl = slice(c * FFN_CHUNK, (c + 1) * FFN_CHUNK)
        for r in range(FFN_ROW_SPLIT):
            g = _dot(xn[r], wg_ref[:, sl])
            u = _dot(xn[r], wu_ref[:, sl])
            a = (g * jax.nn.sigmoid(g) * u).astype(BF16)
            acc[r] = acc[r] + _dot(a, wd_ref[sl, :])
    return [xr + 0.5 * _rms(ar, post_g_ref[...]) for xr, ar in zip(xs, acc)]


def _ffn_in_proj_kernel(x_ref, pre_g_ref, wg_ref, wu_ref, wd_ref, post_g_ref, mix_g_ref, w_ref,
                        h_ref, mqk_ref, mv_ref, fq_ref, fk_ref, fvt_ref, zs_ref):
    hs = _ffn_half_step(x_ref[...], pre_g_ref, wg_ref, wu_ref, wd_ref, post_g_ref)
    rows = x_ref.shape[0] // FFN_ROW_SPLIT
    for r, h in enumerate(hs):
        rs = slice(r * rows, (r + 1) * rows)
        h_ref[rs, :] = h
        u = _rms(h, mix_g_ref[...]).astype(BF16)
        z = _dot(u, w_ref[...])
        o = 2 * MLSTM_QK
        mqk_ref[rs, :] = z[:, :o].astype(BF16)
        mv_ref[rs, :] = z[:, o:o + MLSTM_V].astype(BF16)
        o += MLSTM_V
        fq_ref[rs, :] = (z[:, o:o + FOX_W] * (FOX_DH ** -0.5 * LOG2E)).astype(BF16)
        fk_ref[rs, :] = z[:, o + FOX_W:o + 2 * FOX_W].astype(BF16)
        fvt_ref[0, :, rs] = z[:, o + 2 * FOX_W:o + 3 * FOX_W].T.astype(BF16)
        zs_ref[rs, :] = z[:, o + 3 * FOX_W:]


def _ffn_in_proj(x, ffn, mix_g, w, *, batch, seq):
    t, d = x.shape
    pre_g, w_gate, w_up, w_down, post_g = ffn
    d_ff = w_gate.shape[1]
    n = w.shape[1]
    tiles = seq // TOKEN_TILE
    assert seq % TOKEN_TILE == 0 and d_ff % FFN_CHUNK == 0
    row = lambda width: pl.BlockSpec((TOKEN_TILE, width), lambda i: (i, 0))
    widths = (2 * MLSTM_QK, MLSTM_V, FOX_W, FOX_W)
    return pl.pallas_call(
        _ffn_in_proj_kernel,
        grid=(t // TOKEN_TILE,),
        in_specs=[row(d), _resident((1, d)), _resident((d, d_ff)), _resident((d, d_ff)),
                  _resident((d_ff, d)), _resident((1, d)), _resident((1, d)), _resident((d, n))],
        out_specs=[row(d)] + [row(wd) for wd in widths]
        + [pl.BlockSpec((1, FOX_W, TOKEN_TILE), lambda i: (i // tiles, 0, i % tiles)), row(LANES)],
        out_shape=[jax.ShapeDtypeStruct((t, d), F32)] + [jax.ShapeDtypeStruct((t, wd), BF16) for wd in widths]
        + [jax.ShapeDtypeStruct((batch, FOX_W, seq), BF16), jax.ShapeDtypeStruct((t, LANES), F32)],
        compiler_params=_params("parallel"),
        name="ffn_in_proj",
    )(x, pre_g, w_gate, w_up, w_down, post_g, mix_g, w)


def _ffn_ple_kernel(x_ref, pre_g_ref, wg_ref, wu_ref, wd_ref, post_g_ref,
                    p_ref, ple_pre_g_ref, ple_wg_ref, ple_bg_ref, ple_wp_ref, ple_post_g_ref, o_ref):
    hs = _ffn_half_step(x_ref[...], pre_g_ref, wg_ref, wu_ref, wd_ref, post_g_ref)
    rows = x_ref.shape[0] // FFN_ROW_SPLIT
    for r, h in enumerate(hs):
        rs = slice(r * rows, (r + 1) * rows)
        u = _rms(h, ple_pre_g_ref[...]).astype(BF16)
        gate = jax.nn.sigmoid(_dot(u, ple_wg_ref[...]) + ple_bg_ref[...])
        emb = _dot(p_ref[rs, :].astype(BF16), ple_wp_ref[...])
        o_ref[rs, :] = h + _rms(gate * emb, ple_post_g_ref[...])


def _ffn_ple(x, ffn, p, ple):
    t, d = x.shape
    pre_g, w_gate, w_up, w_down, post_g = ffn
    ple_pre_g, ple_w_g, ple_b_g, ple_w_p, ple_post_g = ple
    d_ff = w_gate.shape[1]
    dp = p.shape[1]
    assert t % TOKEN_TILE == 0 and d_ff % FFN_CHUNK == 0
    row = lambda width: pl.BlockSpec((TOKEN_TILE, width), lambda i: (i, 0))
    return pl.pallas_call(
        _ffn_ple_kernel,
        grid=(t // TOKEN_TILE,),
        in_specs=[row(d), _resident((1, d)), _resident((d, d_ff)), _resident((d, d_ff)),
                  _resident((d_ff, d)), _resident((1, d)),
                  row(dp), _resident((1, d)), _resident((d, d)), _resident((1, d)), _resident((dp, d)),
                  _resident((1, d))],
        out_specs=row(d),
        out_shape=jax.ShapeDtypeStruct((t, d), F32),
        compiler_params=_params("parallel"),
        name="ffn_ple",
    )(x, pre_g, w_gate, w_up, w_down, post_g, p, ple_pre_g, ple_w_g, ple_b_g, ple_w_p, ple_post_g)


def _split3(x):
    hi = x.astype(BF16)
    r1 = x - hi.astype(F32)
    mid = r1.astype(BF16)
    lo = (r1 - mid.astype(F32)).astype(BF16)
    return hi, mid, lo


def _fox_aug_lane(hh, pair):
    return (FOX_DH if hh == 0 else 0) + FOX_AUG_STRIDE * pair


def _routing_constants():
    h_, sl = MLSTM_HEADS, MLSTM_SLOTS
    ra = np.zeros((N_SPLIT * LANES, LANES), np.float32)
    rb = np.zeros((N_SPLIT * LANES, LANES), np.float32)
    ones_a = np.zeros((1, LANES), np.float32)
    ones_b = np.zeros((1, LANES), np.float32)
    bconst = np.zeros((h_, LANES, 2 * LANES), np.float32)
    for h in range(h_):
        for j in range(N_SPLIT):
            ra[j * LANES + h_ + h, sl * h + j] = 1.0
            ra[j * LANES + h, sl * h + N_SPLIT + j] = 1.0
            rb[j * LANES + h, sl * h + 2 * N_SPLIT + j] = 1.0
            rb[j * LANES + h_ + h, sl * h + 3 * N_SPLIT + j] = -1.0
            bconst[h, sl * h + j, :LANES] = 1.0
            bconst[h, sl * h + j, LANES:] = -1.0
            bconst[h, sl * h + N_SPLIT + j, LANES:] = 1.0
        ones_a[0, sl * h + 2 * N_SPLIT:sl * h + 4 * N_SPLIT] = 1.0
        ones_b[0, sl * h:sl * h + N_SPLIT] = 1.0
    rk = np.zeros((N_SPLIT * LANES, LANES), np.float32)
    for h in range(FOX_HEADS):
        for j in range(N_SPLIT):
            rk[j * LANES + 2 * h_ + h, _fox_aug_lane(h % 2, h // 2) + j] = -1.0
    as_bf = lambda a: jnp.asarray(a, BF16)
    return as_bf(ra), jnp.asarray(ones_a), as_bf(rb), jnp.asarray(ones_b), as_bf(rk), as_bf(bconst)


def _gates_kernel(zs_ref, bias_ref, fk_ref, ra_ref, ones_a_ref, rb_ref, ones_b_ref, rk_ref,
                  ga_ref, gb_ref, kaug_ref, carry_ref):
    @pl.when(pl.program_id(1) == 0)
    def _():
        carry_ref[...] = jnp.zeros_like(carry_ref)

    x = zs_ref[0] + bias_ref[...]
    n = x.shape[0]
    log_f = jnp.minimum(x, 0.0) - jnp.log1p(jnp.exp(-jnp.abs(x)))
    L = MLSTM_CHUNK
    tri = jnp.where(lax.broadcasted_iota(jnp.int32, (L, L), 0) >= lax.broadcasted_iota(jnp.int32, (L, L), 1),
                    1.0, 0.0).astype(BF16)
    parts = _split3(log_f)
    cum_chunk = jnp.concatenate(
        [sum(_dot(tri, part[ci * L:(ci + 1) * L, :]) for part in parts) for ci in range(n // L)], axis=0)
    row = lax.broadcasted_iota(jnp.int32, x.shape, 0)
    before = carry_ref[0:1, :]
    offset = jnp.broadcast_to(before, x.shape)
    for ci in range(1, n // MLSTM_CHUNK):
        before = before + cum_chunk[ci * MLSTM_CHUNK - 1:ci * MLSTM_CHUNK, :]
        offset = jnp.where(row >= ci * MLSTM_CHUNK, before, offset)
    cum_all = cum_chunk + offset
    carry_ref[0:1, :] = cum_all[n - 1:n, :]
    col = lax.broadcasted_iota(jnp.int32, x.shape, 1)
    vals = jnp.where(col < MLSTM_HEADS, x, jnp.where(col < 2 * MLSTM_HEADS, cum_chunk, cum_all * LOG2E))

    split = jnp.concatenate(_split3(vals), axis=1)
    ga_ref[0] = (_dot(split, ra_ref[...]) + ones_a_ref[...]).astype(BF16)
    gb_ref[0] = (_dot(split, rb_ref[...]) + ones_b_ref[...]).T.astype(BF16)
    aug = _dot(split, rk_ref[...])
    lane = lax.broadcasted_iota(jnp.int32, (1, LANES), 1)
    lo_half = lane < FOX_DH
    for h in range(FOX_HEADS):
        a0 = _fox_aug_lane(h % 2, h // 2)
        mine = jnp.where(lane >= a0, jnp.where(lane < a0 + N_SPLIT, aug, 0.0), 0.0).astype(BF16)
        k_pair = fk_ref[0, :, (h // 2) * LANES:(h // 2 + 1) * LANES]
        own = lo_half if h % 2 == 0 else jnp.logical_not(lo_half)
        kaug_ref[0, :, h * LANES:(h + 1) * LANES] = jnp.where(own, k_pair, mine)


def _gates(zs, bias, fk, consts):
    b, s, _ = zs.shape
    ra, ones_a, rb, ones_b, rk, _ = consts
    assert s % GATE_TILE == 0 and GATE_TILE % MLSTM_CHUNK == 0
    tile = lambda width: pl.BlockSpec((1, GATE_TILE, width), lambda i, j: (i, j, 0))
    return pl.pallas_call(
        _gates_kernel,
        grid=(b, s // GATE_TILE),
        in_specs=[tile(LANES), pl.BlockSpec((1, LANES), lambda i, j: (0, 0)), tile(FOX_W),
                  _resident(ra.shape), _resident(ones_a.shape), _resident(rb.shape), _resident(ones_b.shape),
                  _resident(rk.shape)],
        out_specs=[tile(LANES), pl.BlockSpec((1, LANES, GATE_TILE), lambda i, j: (i, 0, j)),
                   tile(FOX_HEADS * LANES)],
        out_shape=[jax.ShapeDtypeStruct((b, s, LANES), BF16),
                   jax.ShapeDtypeStruct((b, LANES, s), BF16),
                   jax.ShapeDtypeStruct((b, s, FOX_HEADS * LANES), BF16)],
        scratch_shapes=[pltpu.VMEM((SUBLANES, LANES), F32)],
        compiler_params=_params("parallel", "arbitrary"),
        name="gates",
    )(zs, bias, fk, ra, ones_a, rb, ones_b, rk)


def _mlstm_kernel(qk_ref, v_ref, ga_ref, gb_ref, bconst_ref, cw_ref, cb_ref, ng_ref, o_ref,
                  tail_ref, c_ref, m_ref):
    @pl.when(pl.program_id(1) == 0)
    def _():
        tail_ref[...] = jnp.zeros_like(tail_ref)
        c_ref[...] = jnp.zeros_like(c_ref)
        m_ref[...] = jnp.zeros_like(m_ref)

    L = MLSTM_CHUNK
    rows = qk_ref.shape[1]

    cur = qk_ref[0].astype(F32)
    ext = jnp.concatenate([tail_ref[...], cur], axis=0)
    conv = cb_ref[...] + cw_ref[CONV_WIDTH - 1:CONV_WIDTH, :] * cur
    for d in range(1, CONV_WIDTH):
        shifted = pltpu.roll(ext, d, 0)[SUBLANES:, :]
        conv = conv + cw_ref[CONV_WIDTH - 1 - d:CONV_WIDTH - d, :] * shifted
    tail_ref[...] = cur[rows - SUBLANES:, :]
    qk = conv * jax.nn.sigmoid(conv)
    q_all = qk[:, :MLSTM_QK]
    k_all = qk[:, MLSTM_QK:] * (MLSTM_DQK ** -0.5)

    lane = lax.broadcasted_iota(jnp.int32, (1, LANES), 1)
    lo_half = lane < MLSTM_DQK
    srow = lax.broadcasted_iota(jnp.int32, (LANES, 1), 0) < MLSTM_DQK
    slot_owner = lax.broadcasted_iota(jnp.int32, (LANES, 1), 0) // MLSTM_SLOTS
    tril = (lax.broadcasted_iota(jnp.int32, (L, L), 0) >= lax.broadcasted_iota(jnp.int32, (L, L), 1))
    ones_block = jnp.ones((L, LANES), BF16)
    twice = lambda a: jnp.concatenate([a, a], axis=1)

    chunks = range(rows // L)
    heads = range(MLSTM_HEADS)
    units = [(ci, h) for ci in chunks for h in heads]
    rs = {ci: slice(ci * L, (ci + 1) * L) for ci in chunks}

    b_rep, log_d, m_intra, log_w, w_max, b_last = {}, {}, {}, {}, {}, {}
    for ci, h in units:
        key_side = jnp.where(slot_owner == h, gb_ref[0, :, rs[ci]], jnp.zeros((), BF16))
        g = _dot(ga_ref[0, rs[ci], :], jnp.concatenate([key_side, bconst_ref[h]], axis=1))
        log_d[ci, h] = jnp.where(tril, g[:, :L], -jnp.inf)
        b_rep[ci, h] = g[:, L:L + LANES]
        b_last[ci, h] = b_rep[ci, h][L - 1:L, :]
        m_intra[ci, h] = jnp.max(log_d[ci, h], axis=-1, keepdims=True)
        log_w[ci, h] = b_last[ci, h] + g[:, L + LANES:]
        w_max[ci, h] = jnp.max(log_w[ci, h], axis=0, keepdims=True)

    m_st, m_next, decay = {}, {}, {}
    for h in heads:
        m = m_ref[h:h + 1, :]
        for ci in chunks:
            m_st[ci, h] = m
            m = jnp.maximum(b_last[ci, h] + m, w_max[ci, h])
            m_next[ci, h] = m
            decay[ci, h] = twice(jnp.exp(b_last[ci, h] + m_st[ci, h] - m))
        m_ref[h:h + 1, :] = m

    q_h, k_pair, k_pair_b, v_aug = {}, {}, {}, {}
    for ci in chunks:
        for p in range(MLSTM_HEADS // 2):
            ps = slice(p * LANES, (p + 1) * LANES)
            k_pair[ci, p] = k_all[rs[ci], ps]
            k_pair_b[ci, p] = k_pair[ci, p].astype(BF16)
            q_pair = q_all[rs[ci], ps]
            q_h[ci, 2 * p] = jnp.where(lo_half, q_pair, 0.0).astype(BF16)
            q_h[ci, 2 * p + 1] = jnp.where(lo_half, 0.0, q_pair).astype(BF16)
        for h in heads:
            v_aug[ci, h] = jnp.concatenate([v_ref[0, rs[ci], h * MLSTM_DV:(h + 1) * MLSTM_DV], ones_block], axis=1)

    scores = {u: _dot_nt(q_h[u], k_pair_b[u[0], u[1] // 2]) for u in units}
    upd = {}
    for ci, h in units:
        w = jnp.exp(log_w[ci, h] - m_next[ci, h])
        upd[ci, h] = _dot_tn((k_pair[ci, h // 2] * w).astype(BF16), v_aug[ci, h])

    c_b = {}
    for p in range(MLSTM_HEADS // 2):
        c_pair = c_ref[p]
        for ci in chunks:
            c_b[ci, p] = c_pair.astype(BF16)
            c_pair = jnp.where(srow, decay[ci, 2 * p] * c_pair + upd[ci, 2 * p],
                               decay[ci, 2 * p + 1] * c_pair + upd[ci, 2 * p + 1])
        c_ref[p] = c_pair

    m_t, nd = {}, {}
    for ci, h in units:
        m_t[ci, h] = jnp.maximum(b_rep[ci, h] + m_st[ci, h], m_intra[ci, h])
        sc = (scores[ci, h] * jnp.exp(log_d[ci, h] - m_t[ci, h])).astype(BF16)
        inter = jnp.exp(b_rep[ci, h] + m_st[ci, h] - m_t[ci, h])
        nd[ci, h] = _dot(sc, v_aug[ci, h]) + twice(inter) * _dot(q_h[ci, h], c_b[ci, h // 2])

    hm = {}
    for u in units:
        den = jnp.maximum(jnp.abs(nd[u][:, MLSTM_DV:]), jnp.exp(-m_t[u]))
        hm[u] = nd[u][:, :MLSTM_DV] / den
    cen = {u: hm[u] - jnp.mean(hm[u], axis=-1, keepdims=True) for u in units}
    var = {u: jnp.mean(cen[u] * cen[u], axis=-1, keepdims=True) for u in units}
    for ci, h in units:
        cols = slice(h * MLSTM_DV, (h + 1) * MLSTM_DV)
        o_ref[0, rs[ci], cols] = (cen[ci, h] * lax.rsqrt(var[ci, h] + EPS) * ng_ref[:, cols]).astype(o_ref.dtype)


def _mlstm(mqk, mv, ga, gb, bconst, conv_w, conv_b, norm_g):
    b, s, _ = mqk.shape
    assert s % MLSTM_TILE == 0 and MLSTM_TILE % MLSTM_CHUNK == 0
    assert MLSTM_DV == LANES and MLSTM_HEADS * MLSTM_SLOTS <= LANES
    slab = lambda width: pl.BlockSpec((1, MLSTM_TILE, width), lambda i, j: (i, j, 0))
    return pl.pallas_call(
        _mlstm_kernel,
        grid=(b, s // MLSTM_TILE),
        in_specs=[slab(2 * MLSTM_QK), slab(MLSTM_V), slab(LANES),
                  pl.BlockSpec((1, LANES, MLSTM_TILE), lambda i, j: (i, 0, j)),
                  _resident(bconst.shape),
                  _resident((CONV_WIDTH, 2 * MLSTM_QK)), _resident((1, 2 * MLSTM_QK)),
                  _resident((1, MLSTM_V))],
        out_specs=slab(MLSTM_V),
        out_shape=jax.ShapeDtypeStruct((b, s, MLSTM_V), BF16),
        scratch_shapes=[pltpu.VMEM((SUBLANES, 2 * MLSTM_QK), F32),
                        pltpu.VMEM((MLSTM_HEADS // 2, LANES, 2 * LANES), F32),
                        pltpu.VMEM((SUBLANES, LANES), F32)],
        compiler_params=_params("parallel", "arbitrary"),
        name="mlstm",
    )(mqk, mv, ga, gb, bconst, conv_w, conv_b, norm_g)


def _fox_kernel(q_ref, k_ref, vt_ref, o_ref):
    tq = q_ref.shape[1]
    group = pl.program_id(1)
    i = pl.program_id(2)
    lane = lax.broadcasted_iota(jnp.int32, (1, LANES), 1)
    lo_half = lane < FOX_DH
    heads = range(FOX_STEP_HEADS)
    q_aug = []
    for hh in heads:
        q = q_ref[0, :, (hh // 2) * LANES:(hh // 2 + 1) * LANES]
        a0 = _fox_aug_lane(hh % 2, group * (FOX_STEP_HEADS // 2) + hh // 2)
        bias_lanes = jnp.where(lane >= a0, jnp.where(lane < a0 + N_SPLIT, 1.0, 0.0), 0.0).astype(BF16)
        q_aug.append(jnp.where(lo_half if hh % 2 == 0 else jnp.logical_not(lo_half), q, bias_lanes))

    def scores(hh, off, w):
        return _dot_nt(k_ref[0, off:off + w, hh * LANES:(hh + 1) * LANES], q_aug[hh])

    def absorb(block, hh, off, w, s, state):
        m, l, acc = state
        if off + w > block * tq:
            key_pos = off + lax.broadcasted_iota(jnp.int32, (w, tq), 0)
            query_pos = block * tq + lax.broadcasted_iota(jnp.int32, (w, tq), 1)
            s = jnp.where(key_pos <= query_pos, s, -jnp.inf)
        m_new = jnp.maximum(m, jnp.max(s, axis=0, keepdims=True))
        alpha = jnp.exp2(m - m_new)
        p = jnp.exp2(s - m_new)
        l = alpha * l + jnp.sum(p, axis=0, keepdims=True)
        acc = alpha * acc + _dot(vt_ref[0, hh * FOX_DH:(hh + 1) * FOX_DH, off:off + w], p.astype(BF16))
        return m_new, l, acc

    def sweep(block):
        n_keys = (block + 1) * tq
        pieces = [(off, min(FOX_SUB, n_keys - off)) for off in range(0, n_keys, FOX_SUB)]
        state = [(jnp.full((1, tq), -jnp.inf, F32), jnp.zeros((1, tq), F32), jnp.zeros((FOX_DH, tq), F32))
                 for _ in heads]
        ahead = [scores(hh, *pieces[0]) for hh in heads]
        for k, (off, w) in enumerate(pieces):
            for hh in heads:
                s = ahead[hh]
                if k + 1 < len(pieces):
                    ahead[hh] = scores(hh, *pieces[k + 1])
                state[hh] = absorb(block, hh, off, w, s, state[hh])
        return jnp.concatenate([acc / l for _, l, acc in state], axis=0).T.astype(o_ref.dtype)

    o_ref[0] = lax.switch(i, [lambda block=block: sweep(block) for block in range(k_ref.shape[1] // tq)])


def _fox(fq, kaug, fvt):
    b, s, _ = fq.shape
    assert s % FOX_TQ == 0 and FOX_SUB % FOX_TQ == 0 and 2 * FOX_DH == LANES
    groups = FOX_HEADS // FOX_STEP_HEADS
    width = FOX_STEP_HEADS * FOX_DH
    return pl.pallas_call(
        _fox_kernel,
        grid=(b, groups, s // FOX_TQ),
        in_specs=[pl.BlockSpec((1, FOX_TQ, width), lambda bi, p, i: (bi, i, p)),
                  pl.BlockSpec((1, s, FOX_STEP_HEADS * LANES), lambda bi, p, i: (bi, 0, p)),
                  pl.BlockSpec((1, width, s), lambda bi, p, i: (bi, p, 0))],
        out_specs=pl.BlockSpec((1, FOX_TQ, width), lambda bi, p, i: (bi, i, p)),
        out_shape=jax.ShapeDtypeStruct((b, s, FOX_W), BF16),
        compiler_params=_params("parallel", "parallel", "arbitrary"),
        name="fox",
    )(fq, kaug, fvt)


def _merge_kernel(h_ref, ya_ref, yb_ref, pre_g_ref, wu_ref, bias_ref, wa_ref, wb_ref, wo_ref, post_g_ref, o_ref):
    h = h_ref[...]
    d = h.shape[1]
    u = _rms(h, pre_g_ref[...]).astype(BF16)
    zu = _dot(u, wu_ref[...])
    y_a = (jax.nn.sigmoid(zu[:, :MLSTM_V]) * ya_ref[...].astype(F32)).astype(BF16)
    gates = jax.nn.sigmoid(zu[:, MLSTM_V:] + bias_ref[...])
    merged = gates[:, :d] * _dot(y_a, wa_ref[...]) + gates[:, d:] * _dot(yb_ref[...], wb_ref[...])
    o_ref[...] = h + _rms(_dot(merged.astype(BF16), wo_ref[...]), post_g_ref[...])


def _merge(h, ya, yb, pre_g, w_u, bias, w_a, w_b, w_o, post_g):
    t, d = h.shape
    row = lambda width: pl.BlockSpec((TOKEN_TILE, width), lambda i: (i, 0))
    return pl.pallas_call(
        _merge_kernel,
        grid=(t // TOKEN_TILE,),
        in_specs=[row(d), row(MLSTM_V), row(FOX_W), _resident((1, d)), _resident(w_u.shape),
                  _resident((1, 2 * d)), _resident(w_a.shape), _resident(w_b.shape), _resident((d, d)),
                  _resident((1, d))],
        out_specs=row(d),
        out_shape=jax.ShapeDtypeStruct((t, d), F32),
        compiler_params=_params("parallel"),
        name="merge",
    )(h, ya, yb, pre_g, w_u, bias, w_a, w_b, w_o, post_g)


def _layer(h, p, ffn1_pre_g, ffn1_w_gate, ffn1_w_up, ffn1_w_down, ffn1_post_g,
           mix_pre_g, w_in, conv_w, conv_b, mlstm_i_bias, mlstm_f_bias, mlstm_norm_g,
           fox_f_bias, branch_gate_bias, w_branch_a, w_branch_b, w_out, mix_post_g,
           ffn2_pre_g, ffn2_w_gate, ffn2_w_up, ffn2_w_down, ffn2_post_g,
           ple_pre_g, ple_w_gate, ple_b_gate, ple_w_proj, ple_post_g, *, batch, seq):
    vec = lambda a: a.reshape(1, -1)
    bf = lambda a: a.astype(BF16)

    o_mo = 2 * MLSTM_QK + MLSTM_V
    o_mi = o_mo + MLSTM_V
    o_fq = o_mi + 2 * MLSTM_HEADS
    o_ff = o_fq + 3 * FOX_W
    o_ga = o_ff + FOX_HEADS
    w_gate_cols = jnp.concatenate([w_in[:, o_mi:o_fq], w_in[:, o_ff:o_ga]], axis=1)
    w_gate_cols = jnp.pad(w_gate_cols, ((0, 0), (0, LANES - N_GATES)))
    w_streams = bf(jnp.concatenate([w_in[:, :o_mo], w_in[:, o_fq:o_ff], w_gate_cols], axis=1))
    w_late = bf(jnp.concatenate([w_in[:, o_mo:o_mi], w_in[:, o_ga:]], axis=1))
    gate_bias = jnp.pad(jnp.concatenate([mlstm_i_bias, mlstm_f_bias, fox_f_bias]), (0, LANES - N_GATES))
    consts = _routing_constants()

    ffn1 = (vec(ffn1_pre_g), bf(ffn1_w_gate), bf(ffn1_w_up), bf(ffn1_w_down), vec(ffn1_post_g))
    h1, mqk, mv, fq, fk, fvt, zs = _ffn_in_proj(h, ffn1, vec(mix_pre_g), w_streams, batch=batch, seq=seq)
    per_seq = lambda a: a.reshape(batch, seq, a.shape[-1])
    ga, gb, kaug = _gates(per_seq(zs), vec(gate_bias), per_seq(fk), consts)
    ya = _mlstm(per_seq(mqk), per_seq(mv), ga, gb, consts[-1], conv_w, vec(conv_b), vec(mlstm_norm_g))
    yb = _fox(per_seq(fq), kaug, fvt)

    h2 = _merge(h1, ya.reshape(batch * seq, MLSTM_V), yb.reshape(batch * seq, FOX_W), vec(mix_pre_g), w_late,
                vec(branch_gate_bias), bf(w_branch_a), bf(w_branch_b), bf(w_out), vec(mix_post_g))
    ffn2 = (vec(ffn2_pre_g), bf(ffn2_w_gate), bf(ffn2_w_up), bf(ffn2_w_down), vec(ffn2_post_g))
    ple = (vec(ple_pre_g), bf(ple_w_gate), vec(ple_b_gate), bf(ple_w_proj), vec(ple_post_g))
    return _ffn_ple(h2, ffn2, p, ple)


def kernel(x, p, ffn1_pre_g, ffn1_w_gate, ffn1_w_up, ffn1_w_down, ffn1_post_g, mix_pre_g, w_in, conv_w, conv_b, mlstm_i_bias, mlstm_f_bias, mlstm_norm_g, fox_f_bias, branch_gate_bias, w_branch_a, w_branch_b, w_out, mix_post_g, ffn2_pre_g, ffn2_w_gate, ffn2_w_up, ffn2_w_down, ffn2_post_g, ple_pre_g, ple_w_gate, ple_b_gate, ple_w_proj, ple_post_g):
    batch, seq, d = x.shape
    weights = (ffn1_pre_g, ffn1_w_gate, ffn1_w_up, ffn1_w_down, ffn1_post_g, mix_pre_g, w_in, conv_w, conv_b,
               mlstm_i_bias, mlstm_f_bias, mlstm_norm_g, fox_f_bias, branch_gate_bias, w_branch_a, w_branch_b,
               w_out, mix_post_g, ffn2_pre_g, ffn2_w_gate, ffn2_w_up, ffn2_w_down, ffn2_post_g, ple_pre_g,
               ple_w_gate, ple_b_gate, ple_w_proj, ple_post_g)
    h = x.reshape(batch * seq, d)
    for layer in range(p.shape[0]):
        h = _layer(h, p[layer].reshape(batch * seq, -1), *(w[layer] for w in weights), batch=batch, seq=seq)
    return h.reshape(batch, seq, d)
```
